```python
import math
import jax, jax.numpy as jnp
from jax import lax
import numpy as np

D_MODEL = 2048
BATCH = 4
SEQ = 8192
DEPTH = 2

HEAD_DIM = 128
DIL_GROUPS = ((128, 1), (512, 4), (2048, 16))
N_GROUPS = 3
HEADS_PER_GROUP = 4
DIL_WIDTH = N_GROUPS * HEADS_PER_GROUP * HEAD_DIM
DIL_OUT = HEADS_PER_GROUP * HEAD_DIM
SB_HEADS = 8
SB_WIDTH = SB_HEADS * HEAD_DIM
N_BRANCHES = 2
N_IN = 3 * DIL_WIDTH + 3 * SB_WIDTH + N_BRANCHES * D_MODEL
D_FF = 4 * D_MODEL
ROPE_THETA = 10000.0
BLOCK = 128
EPS = 1e-6

kernel_name = "hybrid_dilated_stickbreak_block"


def rms_norm(x, g):
    xf = x.astype(jnp.float32)
    y = xf * lax.rsqrt(jnp.mean(xf * xf, axis=-1, keepdims=True) + EPS)
    return (y * g.astype(jnp.float32)).astype(x.dtype)


def rotary(x):
    s = x.shape[1]
    half = HEAD_DIM // 2
    inv_freq = ROPE_THETA ** (-jnp.arange(half, dtype=jnp.float32) / half)
    ang = jnp.arange(s, dtype=jnp.float32)[:, None] * inv_freq[None, :]
    cos = jnp.cos(ang)[None, :, None, :]
    sin = jnp.sin(ang)[None, :, None, :]
    xf = x.astype(jnp.float32)
    x1, x2 = xf[..., :half], xf[..., half:]
    out = jnp.concatenate([x1 * cos - x2 * sin, x2 * cos + x1 * sin], axis=-1)
    return out.astype(x.dtype)


def dilated_window_attention(q, k, v, window, dilation):
    b, s, h, d = q.shape
    span = window // dilation
    length = s // dilation
    nb = -(-length // BLOCK)
    lp = nb * BLOCK

    def to_sub(t):
        t = t.reshape(b, length, dilation, h, d).transpose(0, 2, 3, 1, 4)
        t = jnp.pad(t, ((0, 0), (0, 0), (0, 0), (0, lp - length), (0, 0)))
        return t.reshape(b, dilation, h, nb, BLOCK, d)

    def with_prev(t):
        prev = jnp.pad(t[:, :, :, :-1], ((0, 0), (0, 0), (0, 0), (1, 0), (0, 0), (0, 0)))
        return jnp.concatenate([prev, t], axis=4)

    qb = to_sub(q)
    kw = with_prev(to_sub(k))
    vw = with_prev(to_sub(v))
    scores = jnp.einsum('brhnqd,brhnkd->brhnqk', qb, kw).astype(jnp.float32) / math.sqrt(d)
    blk = jnp.arange(nb)
    qi = blk[:, None, None] * BLOCK + jnp.arange(BLOCK)[None, :, None]
    ki = (blk[:, None, None] - 1) * BLOCK + jnp.arange(2 * BLOCK)[None, None, :]
    off = qi - ki
    valid = (off >= 0) & (off <= span) & (ki >= 0)
    scores = jnp.where(valid, scores, -jnp.inf)
    m = jnp.max(scores, axis=-1, keepdims=True)
    p = jnp.exp(scores - m)
    l = jnp.sum(p, axis=-1, keepdims=True)
    o = jnp.einsum('brhnqk,brhnkd->brhnqd', (p / l).astype(v.dtype), vw)
    log_den = (m + jnp.log(l))[..., 0]
    o = o.reshape(b, dilation, h, lp, d)[:, :, :, :length]
    o = o.transpose(0, 3, 1, 2, 4).reshape(b, s, h, d)
    log_den = log_den.reshape(b, dilation, h, lp)[..., :length]
    log_den = log_den.transpose(0, 3, 1, 2).reshape(b, s, h)
    return o, log_den


def stick_breaking_attention(q, k, v):
    b, s, h, d = q.shape
    nb = s // BLOCK
    qb = q.reshape(b, nb, BLOCK, h, d).transpose(1, 0, 3, 2, 4)
    kt = k.transpose(0, 2, 1, 3)
    vt = v.transpose(0, 2, 1, 3)
    key_pos = jnp.arange(s)

    def block(args):
        q_blk, start = args
        z = jnp.einsum('bhqd,bhkd->bhqk', q_blk, kt).astype(jnp.float32) / math.sqrt(d)
        qpos = start + jnp.arange(BLOCK)
        mask = key_pos[None, :] < qpos[:, None]
        log_keep = jnp.where(mask, -jax.nn.softplus(z), 0.0)
        between = lax.cumsum(log_keep, axis=3, reverse=True) - log_keep
        a = jnp.where(mask, jnp.exp(jax.nn.log_sigmoid(z) + between), 0.0)
        return jnp.einsum('bhqk,bhkd->bhqd', a.astype(vt.dtype), vt)

    starts = jnp.arange(nb) * BLOCK
    o = lax.map(block, (qb, starts))
    return o.transpose(1, 0, 3, 2, 4).reshape(b, s, h, d)


def setup_inputs(seed: int = 0) -> dict:
    key = jax.random.key(seed)
    ks = jax.random.split(key, 13)
    f32 = jnp.float32

    def nrm(k, shape, fan_in):
        return jax.random.normal(k, shape, f32) * (fan_in ** -0.5)

    def gain(k, shape):
        return 1.0 + 0.02 * jax.random.normal(k, shape, f32)

    return {
        "x": jax.random.normal(ks[0], (BATCH, SEQ, D_MODEL), f32),
        "norm1_g": gain(ks[1], (DEPTH, D_MODEL)),
        "w_in": nrm(ks[2], (DEPTH, D_MODEL, N_IN), D_MODEL),
        "q_norm_g": gain(ks[3], (DEPTH, N_GROUPS, HEAD_DIM)),
        "k_norm_g": gain(ks[4], (DEPTH, N_GROUPS, HEAD_DIM)),
        "w_up_dil": nrm(ks[5], (DEPTH, DIL_OUT, D_MODEL), DIL_OUT),
        "w_up_sb": nrm(ks[6], (DEPTH, SB_WIDTH, D_MODEL), SB_WIDTH),
        "gate_b": 0.01 * jax.random.normal(ks[7], (DEPTH, N_BRANCHES, D_MODEL), f32),
        "w_out": nrm(ks[8], (DEPTH, D_MODEL, D_MODEL), D_MODEL),
        "norm2_g": gain(ks[9], (DEPTH, D_MODEL)),
        "w_ff1": nrm(ks[10], (DEPTH, D_MODEL, D_FF), D_MODEL),
        "w_ff2": nrm(ks[11], (DEPTH, D_FF, D_MODEL), D_FF),
    }


def reference(x, norm1_g, w_in, q_norm_g, k_norm_g, w_up_dil, w_up_sb, gate_b, w_out, norm2_g, w_ff1, w_ff2):
    b, s, _ = x.shape
    cuts = [DIL_WIDTH, 2 * DIL_WIDTH, 3 * DIL_WIDTH,
            3 * DIL_WIDTH + SB_WIDTH, 3 * DIL_WIDTH + 2 * SB_WIDTH, 3 * DIL_WIDTH + 3 * SB_WIDTH]
    for layer in range(DEPTH):
        h = rms_norm(x, norm1_g[layer])
        proj = h @ w_in[layer]
        q_d, k_d, v_d, q_s, k_s, v_s, g_pre = jnp.split(proj, cuts, axis=-1)
        q_d = q_d.reshape(b, s, N_GROUPS, HEADS_PER_GROUP, HEAD_DIM)
        k_d = k_d.reshape(b, s, N_GROUPS, HEADS_PER_GROUP, HEAD_DIM)
        v_d = v_d.reshape(b, s, N_GROUPS, HEADS_PER_GROUP, HEAD_DIM)

        outs, dens = [], []
        for g, (window, dilation) in enumerate(DIL_GROUPS):
            qg = rotary(rms_norm(q_d[:, :, g], q_norm_g[layer, g]))
            kg = rotary(rms_norm(k_d[:, :, g], k_norm_g[layer, g]))
            o_g, den_g = dilated_window_attention(qg, kg, v_d[:, :, g], window, dilation)
            outs.append(o_g)
            dens.append(den_g)
        wts = jax.nn.softmax(jnp.stack(dens, axis=0), axis=0)
        y_dil = jnp.sum(wts[..., None] * jnp.stack(outs, axis=0).astype(jnp.float32), axis=0)
        y_dil = y_dil.astype(x.dtype).reshape(b, s, DIL_OUT)

        y_sb = stick_breaking_attention(
            q_s.reshape(b, s, SB_HEADS, HEAD_DIM),
            k_s.reshape(b, s, SB_HEADS, HEAD_DIM),
            v_s.reshape(b, s, SB_HEADS, HEAD_DIM),
        ).reshape(b, s, SB_WIDTH)

        gates = jax.nn.sigmoid(g_pre.reshape(b, s, N_BRANCHES, D_MODEL) + gate_b[layer])
        mixed = gates[:, :, 0] * (y_dil @ w_up_dil[layer]) + gates[:, :, 1] * (y_sb @ w_up_sb[layer])
        x = x + mixed @ w_out[layer]

        h2 = rms_norm(x, norm2_g[layer])
        x = x + jnp.square(jax.nn.relu(h2 @ w_ff1[layer])) @ w_ff2[layer]
    return x
```

```python
import functools
import math

import jax
import jax.numpy as jnp
from jax import lax
from jax.experimental import pallas as pl
from jax.experimental.pallas import tpu as pltpu

D_MODEL = 2048
DEPTH = 2
HEAD_DIM = 128
DIL_GROUPS = ((128, 1), (512, 4), (2048, 16))
N_GROUPS = 3
HEADS_PER_GROUP = 4
GROUP_WIDTH = HEADS_PER_GROUP * HEAD_DIM
DIL_WIDTH = N_GROUPS * GROUP_WIDTH
SB_HEADS = 8
SB_WIDTH = SB_HEADS * HEAD_DIM
QKV_WIDTH = 3 * DIL_WIDTH + 3 * SB_WIDTH
GATE_WIDTH = 2 * D_MODEL
D_FF = 4 * D_MODEL
ROPE_THETA = 10000.0
BLOCK = 128
EPS = 1e-6
INV_SQRT_D = 1.0 / math.sqrt(HEAD_DIM)

F32 = jnp.float32
BF16 = jnp.bfloat16

VMEM_LIMIT = 56 * 1024 * 1024


def _params(semantics):
    return pltpu.CompilerParams(dimension_semantics=semantics, vmem_limit_bytes=VMEM_LIMIT)


def _rms_norm(x, g):
    ms = jnp.mean(x * x, axis=-1, keepdims=True)
    return x * lax.rsqrt(ms + EPS) * g


PROJ_TM = 1024
PROJ_TN = 512
N_QKV_BLOCKS = QKV_WIDTH // PROJ_TN
N_GATE_BLOCKS = GATE_WIDTH // PROJ_TN


def _norm_proj_kernel(x_ref, g_ref, w_ref, qkv_ref, gate_ref, h_ref):
    j = pl.program_id(1)

    @pl.when(j == 0)
    def _():
        h_ref[...] = _rms_norm(x_ref[...], g_ref[...]).astype(BF16)

    acc = jnp.dot(h_ref[...], w_ref[...], preferred_element_type=F32).astype(BF16)

    @pl.when(j < N_QKV_BLOCKS)
    def _():
        qkv_ref[...] = acc

    @pl.when(j >= N_QKV_BLOCKS)
    def _():
        gate_ref[...] = acc


def _norm_proj(x, g, w):
    t = x.shape[0]
    return pl.pallas_call(
        _norm_proj_kernel,
        grid=(t // PROJ_TM, N_QKV_BLOCKS + N_GATE_BLOCKS),
        in_specs=[
            pl.BlockSpec((PROJ_TM, D_MODEL), lambda i, j: (i, 0)),
            pl.BlockSpec((1, D_MODEL), lambda i, j: (0, 0)),
            pl.BlockSpec((D_MODEL, PROJ_TN), lambda i, j: (0, j)),
        ],
        out_specs=[
            pl.BlockSpec((PROJ_TM, PROJ_TN), lambda i, j: (i, jnp.minimum(j, N_QKV_BLOCKS - 1))),
            pl.BlockSpec((PROJ_TM, PROJ_TN), lambda i, j: (i, jnp.maximum(j - N_QKV_BLOCKS, 0))),
        ],
        out_shape=[
            jax.ShapeDtypeStruct((t, QKV_WIDTH), BF16),
            jax.ShapeDtypeStruct((t, GATE_WIDTH), BF16),
        ],
        scratch_shapes=[pltpu.VMEM((PROJ_TM, D_MODEL), BF16)],
        compiler_params=_params(("parallel", "arbitrary")),
    )(x, g, w)


def _norm_rope(x, gain, cos, sin):
    y = _rms_norm(x, gain)
    return y * cos + pltpu.roll(y, HEAD_DIM // 2, 1) * sin


def _dil_kernel(q_ref, kc_ref, kp_ref, vc_ref, vp_ref, csc_ref, csp_ref, qg_ref, kg_ref,
                o_ref, ld_ref):
    n = pl.program_id(2)
    cos_c = csc_ref[:, :HEAD_DIM]
    sin_c = csc_ref[:, HEAD_DIM:]
    cos_k = jnp.concatenate([csp_ref[:, :HEAD_DIM], cos_c], axis=0)
    sin_k = jnp.concatenate([csp_ref[:, HEAD_DIM:], sin_c], axis=0)
    qg = qg_ref[...]
    kg = kg_ref[...]

    row = lax.broadcasted_iota(jnp.int32, (BLOCK, 2 * BLOCK), 0)
    col = lax.broadcasted_iota(jnp.int32, (BLOCK, 2 * BLOCK), 1)
    off = BLOCK + row - col
    valid = (off >= 0) & (off <= BLOCK) & ((col >= BLOCK) | (n > 0))

    for hh in range(HEADS_PER_GROUP):
        sl = slice(hh * HEAD_DIM, (hh + 1) * HEAD_DIM)
        q = _norm_rope(q_ref[:, sl].astype(F32), qg, cos_c, sin_c).astype(BF16)
        k_raw = jnp.concatenate([kp_ref[:, sl], kc_ref[:, sl]], axis=0).astype(F32)
        k = _norm_rope(k_raw, kg, cos_k, sin_k).astype(BF16)
        v = jnp.concatenate([vp_ref[:, sl], vc_ref[:, sl]], axis=0)
        s = lax.dot_general(q, k, (((1,), (1,)), ((), ())), preferred_element_type=F32)
        s = jnp.where(valid, s * INV_SQRT_D, -jnp.inf)
        m = jnp.max(s, axis=-1, keepdims=True)
        p = jnp.exp(s - m)
        l = jnp.sum(p, axis=-1, keepdims=True)
        o = jnp.dot((p / l).astype(BF16), v, preferred_element_type=F32)
        o_ref[:, sl] = o
        ld_ref[:, sl] = jnp.broadcast_to(m + jnp.log(l), (BLOCK, HEAD_DIM))


def _dilated_attention(qkv, cs, q_gain, k_gain, group, batch, seq):
    _, r = DIL_GROUPS[group]
    length = seq // r
    nb = length // BLOCK
    qkv_v = qkv.reshape(batch, length, r * QKV_WIDTH)
    cs_v = cs.reshape(length, r * 2 * HEAD_DIM)
    per_row = QKV_WIDTH // GROUP_WIDTH
    k_off = DIL_WIDTH // GROUP_WIDTH + group
    v_off = 2 * DIL_WIDTH // GROUP_WIDTH + group

    def blk(off, prev):
        if prev:
            return pl.BlockSpec((None, BLOCK, GROUP_WIDTH),
                                lambda b, c, n: (b, jnp.maximum(n - 1, 0), c * per_row + off))
        return pl.BlockSpec((None, BLOCK, GROUP_WIDTH), lambda b, c, n: (b, n, c * per_row + off))

    out_spec = pl.BlockSpec((None, BLOCK, GROUP_WIDTH), lambda b, c, n: (b, n, c))
    out_sds = jax.ShapeDtypeStruct((batch, length, r * GROUP_WIDTH), F32)
    o, ld = pl.pallas_call(
        _dil_kernel,
        grid=(batch, r, nb),
        in_specs=[
            blk(group, False), blk(k_off, False), blk(k_off, True), blk(v_off, False), blk(v_off, True),
            pl.BlockSpec((BLOCK, 2 * HEAD_DIM), lambda b, c, n: (n, c)),
            pl.BlockSpec((BLOCK, 2 * HEAD_DIM), lambda b, c, n: (jnp.maximum(n - 1, 0), c)),
            pl.BlockSpec((1, HEAD_DIM), lambda b, c, n: (0, 0)),
            pl.BlockSpec((1, HEAD_DIM), lambda b, c, n: (0, 0)),
        ],
        out_specs=[out_spec, out_spec],
        out_shape=[out_sds, out_sds],
        compiler_params=_params(("parallel", "parallel", "arbitrary")),
    )(qkv_v, qkv_v, qkv_v, qkv_v, qkv_v, cs_v, cs_v, q_gain, k_gain)
    return o.reshape(batch * seq, GROUP_WIDTH), ld.reshape(batch * seq, GROUP_WIDTH)


SB_TQ = 128
SB_TK = 128


def _softplus(z):
    return jnp.maximum(z, 0.0) + jnp.log1p(jnp.exp(-jnp.abs(z)))


def _sb_kernel(q_ref, k_ref, v_ref, o_ref):
    i = pl.program_id(2)
    q = q_ref[...]
    tj = lax.broadcasted_iota(jnp.int32, (SB_TK, SB_TK), 0)
    ts = lax.broadcasted_iota(jnp.int32, (SB_TK, SB_TK), 1)
    tri = jnp.where(tj >= ts, 1.0, 0.0).astype(BF16)

    def tile(j, carry, acc, mask):
        start = pl.multiple_of(j * SB_TK, SB_TK)
        k = k_ref[pl.ds(start, SB_TK), :]
        v = v_ref[pl.ds(start, SB_TK), :]
        z = lax.dot_general(q, k, (((1,), (1,)), ((), ())), preferred_element_type=F32) * INV_SQRT_D
        lk = -_softplus(z)
        if mask is not None:
            lk = jnp.where(mask, lk, 0.0)
        lk_hi = lk.astype(BF16)
        lk_lo = (lk - lk_hi.astype(F32)).astype(BF16)
        suffix = (jnp.dot(lk_hi, tri, preferred_element_type=F32)
                  + jnp.dot(lk_lo, tri, preferred_element_type=F32))
        a = jnp.exp(z + suffix + carry)
        if mask is not None:
            a = jnp.where(mask, a, 0.0)
        acc = acc + jnp.dot(a.astype(BF16), v, preferred_element_type=F32)
        carry = carry + jnp.sum(lk, axis=-1, keepdims=True)
        return carry, acc

    qi = lax.broadcasted_iota(jnp.int32, (SB_TQ, SB_TK), 0)
    ki = lax.broadcasted_iota(jnp.int32, (SB_TQ, SB_TK), 1)
    carry0 = jnp.zeros((SB_TQ, 1), F32)
    acc0 = jnp.zeros((SB_TQ, HEAD_DIM), F32)
    carry, acc = tile(i, carry0, acc0, ki < qi)

    def body(jj, state):
        return tile(i - 1 - jj, state[0], state[1], None)

    _, acc = lax.fori_loop(0, i, body, (carry, acc))
    o_ref[...] = acc.astype(o_ref.dtype)


def _stick_breaking_attention(qkv, batch, seq):
    qkv_v = qkv.reshape(batch, seq, QKV_WIDTH)
    base = 3 * DIL_WIDTH // HEAD_DIM
    return pl.pallas_call(
        _sb_kernel,
        grid=(batch, SB_HEADS, seq // SB_TQ),
        in_specs=[
            pl.BlockSpec((None, SB_TQ, HEAD_DIM), lambda b, h, i: (b, i, base + h)),
            pl.BlockSpec((None, seq, HEAD_DIM), lambda b, h, i: (b, 0, base + SB_HEADS + h)),
            pl.BlockSpec((None, seq, HEAD_DIM), lambda b, h, i: (b, 0, base + 2 * SB_HEADS + h)),
        ],
        out_specs=pl.BlockSpec((None, SB_TQ, HEAD_DIM), lambda b, h, i: (b, i, h)),
        out_shape=jax.ShapeDtypeStruct((batch, seq, SB_WIDTH), BF16),
        compiler_params=_params(("parallel", "parallel", "arbitrary")),
    )(qkv_v, qkv_v, qkv_v).reshape(batch * seq, SB_WIDTH)


MIX_TM = 256


def _mix_kernel(x_ref, o0_ref, o1_ref, o2_ref, l0_ref, l1_ref, l2_ref, ysb_ref, gate_ref, gb_ref,
                wud_ref, wus_ref, wo_ref, out_ref):
    l0, l1, l2 = l0_ref[...], l1_ref[...], l2_ref[...]
    m = jnp.maximum(jnp.maximum(l0, l1), l2)
    e0, e1, e2 = jnp.exp(l0 - m), jnp.exp(l1 - m), jnp.exp(l2 - m)
    y_dil = (e0 * o0_ref[...] + e1 * o1_ref[...] + e2 * o2_ref[...]) / (e0 + e1 + e2)
    up_dil = jnp.dot(y_dil.astype(BF16), wud_ref[...], preferred_element_type=F32)
    up_sb = jnp.dot(ysb_ref[...], wus_ref[...], preferred_element_type=F32)
    gates = jax.nn.sigmoid(gate_ref[...].astype(F32) + gb_ref[...])
    mixed = gates[:, :D_MODEL] * up_dil + gates[:, D_MODEL:] * up_sb
    out_ref[...] = x_ref[...] + jnp.dot(mixed.astype(BF16), wo_ref[...], preferred_element_type=F32)


def _mix(x, outs, lds, y_sb, gates, gate_b, w_up_dil, w_up_sb, w_out):
    t = x.shape[0]
    rows = lambda w: pl.BlockSpec((MIX_TM, w), lambda i: (i, 0))
    full = lambda a: pl.BlockSpec(a.shape, lambda i: (0, 0))
    return pl.pallas_call(
        _mix_kernel,
        grid=(t // MIX_TM,),
        in_specs=[rows(D_MODEL)] + [rows(GROUP_WIDTH)] * 6 + [rows(SB_WIDTH), rows(GATE_WIDTH),
                                                              full(gate_b), full(w_up_dil),
                                                              full(w_up_sb), full(w_out)],
        out_specs=rows(D_MODEL),
        out_shape=jax.ShapeDtypeStruct((t, D_MODEL), F32),
        compiler_params=_params(("parallel",)),
    )(x, *outs, *lds, y_sb, gates, gate_b, w_up_dil, w_up_sb, w_out)


MLP_TM = 512
MLP_TF = 512


def _mlp_kernel(x_ref, g_ref, w1_ref, w2_ref, o_ref, h_ref):
    j = pl.program_id(1)

    @pl.when(j == 0)
    def _():
        x = x_ref[...]
        h_ref[...] = _rms_norm(x, g_ref[...]).astype(BF16)
        o_ref[...] = x

    u = jnp.maximum(jnp.dot(h_ref[...], w1_ref[...], preferred_element_type=F32), 0.0)
    o_ref[...] += jnp.dot((u * u).astype(BF16), w2_ref[...], preferred_element_type=F32)


def _mlp(x, g, w1, w2):
    t = x.shape[0]
    return pl.pallas_call(
        _mlp_kernel,
        grid=(t // MLP_TM, D_FF // MLP_TF),
        in_specs=[
            pl.BlockSpec((MLP_TM, D_MODEL), lambda i, j: (i, 0)),
            pl.BlockSpec((1, D_MODEL), lambda i, j: (0, 0)),
            pl.BlockSpec((D_MODEL, MLP_TF), lambda i, j: (0, j)),
            pl.BlockSpec((MLP_TF, D_MODEL), lambda i, j: (j, 0)),
        ],
        out_specs=pl.BlockSpec((MLP_TM, D_MODEL), lambda i, j: (i, 0)),
        out_shape=jax.ShapeDtypeStruct((t, D_MODEL), F32),
        scratch_shapes=[pltpu.VMEM((MLP_TM, D_MODEL), BF16)],
        compiler_params=_params(("parallel", "arbitrary")),
    )(x, g, w1, w2)


def _rope_table(seq):
    half = HEAD_DIM // 2
    inv_freq = ROPE_THETA ** (-jnp.arange(half, dtype=F32) / half)
    ang = jnp.arange(seq, dtype=F32)[:, None] * inv_freq[None, :]
    cos, sin = jnp.cos(ang), jnp.sin(ang)
    return jnp.concatenate([cos, cos, -sin, sin], axis=-1)


def kernel(x, norm1_g, w_in, q_norm_g, k_norm_g, w_up_dil, w_up_sb, gate_b, w_out, norm2_g, w_ff1, w_ff2):
    batch, seq, d = x.shape
    xf = x.reshape(batch * seq, d)
    cs = _rope_table(seq)
    for layer in range(DEPTH):
        qkv, gates = _norm_proj(xf, norm1_g[layer][None], w_in[layer].astype(BF16))
        outs, lds = [], []
        for group in range(N_GROUPS):
            o, ld = _dilated_attention(qkv, cs, q_norm_g[layer, group][None], k_norm_g[layer, group][None],
                                       group, batch, seq)
            outs.append(o)
            lds.append(ld)
        y_sb = _stick_breaking_attention(qkv, batch, seq)
        xf = _mix(xf, outs, lds, y_sb, gates, gate_b[layer].reshape(1, GATE_WIDTH),
                  w_up_dil[layer].astype(BF16), w_up_sb[layer].astype(BF16), w_out[layer].astype(BF16))
        xf = _mlp(xf, norm2_g[layer][None], w_ff1[layer].astype(BF16), w_ff2[layer].astype(BF16))
    return xf.reshape(batch, seq, d)
```

```python
import functools
import math

import jax
import jax.numpy as jnp
from jax import lax
from jax.experimental import pallas as pl
from jax.experimental.pallas import tpu as pltpu

D_MODEL = 2048
DEPTH = 2
HEAD_DIM = 128
DIL_GROUPS = ((128, 1), (512, 4), (2048, 16))
N_GROUPS = 3
HEADS_PER_GROUP = 4
GROUP_WIDTH = HEADS_PER_GROUP * HEAD_DIM
DIL_WIDTH = N_GROUPS * GROUP_WIDTH
SB_HEADS = 8
SB_WIDTH = SB_HEADS * HEAD_DIM
GATE_WIDTH = 2 * D_MODEL
D_FF = 4 * D_MODEL
ROPE_THETA = 10000.0
BLOCK = 128
EPS = 1e-6
INV_SQRT_D = 1.0 / math.sqrt(HEAD_DIM)

F32 = jnp.float32
BF16 = jnp.bfloat16

VMEM_LIMIT = 56 * 1024 * 1024


def _params(semantics):
    return pltpu.CompilerParams(dimension_semantics=semantics, vmem_limit_bytes=VMEM_LIMIT)


def _rms_norm(x, g):
    ms = jnp.mean(x * x, axis=-1, keepdims=True)
    return x * lax.rsqrt(ms + EPS) * g


PROJ_TM = 1024
PROJ_TN = 512
N_ROLES = 3
SB_BLOCK0 = N_GROUPS * N_ROLES
GATE_BLOCK0 = SB_BLOCK0 + 3 * SB_WIDTH // PROJ_TN
N_PROJ_BLOCKS = GATE_BLOCK0 + GATE_WIDTH // PROJ_TN


def _norm_proj_kernel(x_ref, g_ref, w_ref, d0_ref, d1_ref, d2_ref, sb_ref, gate_ref, h_ref, acc_ref):
    j = pl.program_id(1)

    @pl.when(j == 0)
    def _():
        h_ref[...] = _rms_norm(x_ref[...], g_ref[...]).astype(BF16)

    acc = jnp.dot(h_ref[...], w_ref[...], preferred_element_type=F32)

    @pl.when(j < N_ROLES)
    def _():
        d0_ref[...] = acc.astype(BF16)

    for group, d_ref in ((1, d1_ref), (2, d2_ref)):
        r = DIL_GROUPS[group][1]

        @pl.when((j >= group * N_ROLES) & (j < (group + 1) * N_ROLES))
        def _(r=r, d_ref=d_ref):
            for lc in range(PROJ_TN // BLOCK):
                acc_ref[lc] = acc[:, lc * BLOCK:(lc + 1) * BLOCK]
            for c in range(r):
                for lc in range(PROJ_TN // BLOCK):
                    d_ref[:, c * PROJ_TN + lc * BLOCK: c * PROJ_TN + (lc + 1) * BLOCK] = (
                        acc_ref[lc, pl.ds(c, PROJ_TM // r, stride=r), :].astype(BF16))

    @pl.when((j >= SB_BLOCK0) & (j < GATE_BLOCK0))
    def _():
        sb_ref[...] = acc.astype(BF16)

    @pl.when(j >= GATE_BLOCK0)
    def _():
        gate_ref[...] = acc.astype(BF16)


def _held(j, first, count):
    return jnp.clip(j - first, 0, count - 1)


def _norm_proj(x, g, w):
    t = x.shape[0]

    def dil_spec(group):
        r = DIL_GROUPS[group][1]
        return pl.BlockSpec((None, PROJ_TM // r, r * PROJ_TN),
                            lambda i, j: (_held(j, group * N_ROLES, N_ROLES), i, 0))

    def dil_shape(group):
        r = DIL_GROUPS[group][1]
        return jax.ShapeDtypeStruct((N_ROLES, t // r, r * GROUP_WIDTH), BF16)

    return pl.pallas_call(
        _norm_proj_kernel,
        grid=(t // PROJ_TM, N_PROJ_BLOCKS),
        in_specs=[
            pl.BlockSpec((PROJ_TM, D_MODEL), lambda i, j: (i, 0)),
            pl.BlockSpec((1, D_MODEL), lambda i, j: (0, 0)),
            pl.BlockSpec((D_MODEL, PROJ_TN), lambda i, j: (0, j)),
        ],
        out_specs=[
            dil_spec(0), dil_spec(1), dil_spec(2),
            pl.BlockSpec((PROJ_TM, PROJ_TN), lambda i, j: (i, _held(j, SB_BLOCK0, GATE_BLOCK0 - SB_BLOCK0))),
            pl.BlockSpec((PROJ_TM, PROJ_TN), lambda i, j: (i, _held(j, GATE_BLOCK0, N_PROJ_BLOCKS - GATE_BLOCK0))),
        ],
        out_shape=[
            dil_shape(0), dil_shape(1), dil_shape(2),
            jax.ShapeDtypeStruct((t, 3 * SB_WIDTH), BF16),
            jax.ShapeDtypeStruct((t, GATE_WIDTH), BF16),
        ],
        scratch_shapes=[pltpu.VMEM((PROJ_TM, D_MODEL), BF16),
                        pltpu.VMEM((PROJ_TN // BLOCK, PROJ_TM, BLOCK), F32)],
        compiler_params=_params(("parallel", "arbitrary")),
        name="norm_proj",
    )(x, g, w)


def _reorder_w_in(w):
    cols = [w[:, role * DIL_WIDTH + group * GROUP_WIDTH: role * DIL_WIDTH + (group + 1) * GROUP_WIDTH]
            for group in range(N_GROUPS) for role in range(N_ROLES)]
    return jnp.concatenate(cols + [w[:, N_ROLES * DIL_WIDTH:]], axis=1).astype(BF16)


DIL_SHAPE = ((4, 1), (1, 4), (1, 4))


def _norm_rope(x, gain, cos, sin):
    y = _rms_norm(x, gain)
    return y * cos + pltpu.roll(y, HEAD_DIM // 2, 1) * sin


def _dil_kernel(q_ref, kc_ref, kp_ref, vc_ref, vp_ref, csc_ref, csp_ref, qg_ref, kg_ref,
                o_ref, ld_ref, *, r, n_sub, n_res):
    n = pl.program_id(1)
    cc = pl.program_id(2)
    qg = qg_ref[...]
    kg = kg_ref[...]

    row = lax.broadcasted_iota(jnp.int32, (BLOCK, 2 * BLOCK), 0)
    col = lax.broadcasted_iota(jnp.int32, (BLOCK, 2 * BLOCK), 1)
    off = BLOCK + row - col
    band = (off >= 0) & (off <= BLOCK)
    band_first = band & ((col >= BLOCK) | (n > 0))

    for cl in range(n_res):
        cos_sl = slice(cl * 2 * HEAD_DIM, cl * 2 * HEAD_DIM + HEAD_DIM)
        sin_sl = slice(cl * 2 * HEAD_DIM + HEAD_DIM, (cl + 1) * 2 * HEAD_DIM)
        cos_c = csc_ref[:, cos_sl]
        sin_c = csc_ref[:, sin_sl]
        cos_k = jnp.concatenate([csp_ref[:, cos_sl], cos_c], axis=0)
        sin_k = jnp.concatenate([csp_ref[:, sin_sl], sin_c], axis=0)
        for hh in range(HEADS_PER_GROUP):
            sl = slice(cl * GROUP_WIDTH + hh * HEAD_DIM, cl * GROUP_WIDTH + (hh + 1) * HEAD_DIM)
            q = _norm_rope(q_ref[:, sl].astype(F32), qg, cos_c, sin_c).astype(BF16)
            k_raw = jnp.concatenate([kp_ref[:, sl], kc_ref[:, sl]], axis=0).astype(F32)
            k = _norm_rope(k_raw, kg, cos_k, sin_k).astype(BF16)
            v = jnp.concatenate([vp_ref[:, sl], vc_ref[:, sl]], axis=0)
            for m in range(n_sub):
                rows = slice(m * BLOCK, (m + 1) * BLOCK)
                keys = slice(m * BLOCK, (m + 2) * BLOCK)
                s = lax.dot_general(q[rows], k[keys], (((1,), (1,)), ((), ())), preferred_element_type=F32)
                s = jnp.where(band_first if m == 0 else band, s * INV_SQRT_D, -jnp.inf)
                mx = jnp.max(s, axis=-1, keepdims=True)
                p = jnp.exp(s - mx)
                l = jnp.sum(p, axis=-1, keepdims=True)
                o = jnp.dot((p / l).astype(BF16), v[keys], preferred_element_type=F32)
                ld = jnp.broadcast_to(mx + jnp.log(l), (BLOCK, HEAD_DIM))
                if r == 1:
                    tok = rows
                else:
                    tok = pl.ds(m * BLOCK * r + cc * n_res + cl, BLOCK, stride=r)
                o_ref[hh, tok, :] = o
                ld_ref[hh, tok, :] = ld


def _dilated_attention(d, cs, q_gain, k_gain, group, batch, seq):
    _, r = DIL_GROUPS[group]
    n_sub, n_res = DIL_SHAPE[group]
    rows = n_sub * BLOCK
    nb = seq // r // rows
    cs_v = cs.reshape(seq // r, r * 2 * HEAD_DIM)
    width = n_res * GROUP_WIDTH

    def cur(role):
        return pl.BlockSpec((None, rows, width), lambda b, n, c: (role, b * nb + n, c))

    def prev(role):
        return pl.BlockSpec((None, BLOCK, width),
                            lambda b, n, c: (role, jnp.maximum((b * nb + n) * n_sub - 1, 0), c))

    out_spec = pl.BlockSpec((HEADS_PER_GROUP, rows * r, HEAD_DIM), lambda b, n, c: (0, b * nb + n, 0))
    out_sds = jax.ShapeDtypeStruct((HEADS_PER_GROUP, batch * seq, HEAD_DIM), F32)
    return pl.pallas_call(
        functools.partial(_dil_kernel, r=r, n_sub=n_sub, n_res=n_res),
        grid=(batch, nb, r // n_res),
        in_specs=[
            cur(0), cur(1), prev(1), cur(2), prev(2),
            pl.BlockSpec((rows, n_res * 2 * HEAD_DIM), lambda b, n, c: (n, c)),
            pl.BlockSpec((BLOCK, n_res * 2 * HEAD_DIM), lambda b, n, c: (jnp.maximum(n * n_sub - 1, 0), c)),
            pl.BlockSpec((1, HEAD_DIM), lambda b, n, c: (0, 0)),
            pl.BlockSpec((1, HEAD_DIM), lambda b, n, c: (0, 0)),
        ],
        out_specs=[out_spec, out_spec],
        out_shape=[out_sds, out_sds],
        compiler_params=_params(("parallel", "parallel", "arbitrary")),
        name=f"dilated_r{r}",
    )(d, d, d, d, d, cs_v, cs_v, q_gain, k_gain)


SB_T = 128
SB_CHAINS = 4
SB_TQ = SB_CHAINS * SB_T
SB_STATIC_TILES = 2
SB_DONE = -104.0
SB_SKIP = -1e30


def _sb_kernel(q_ref, k_ref, v_ref, o_ref):
    i = pl.program_id(2)
    r0 = i * SB_TQ
    tj = lax.broadcasted_iota(jnp.int32, (2 * SB_T, 2 * SB_T), 0) % SB_T
    ts = lax.broadcasted_iota(jnp.int32, (2 * SB_T, 2 * SB_T), 1)
    sum_rhs = jnp.where((tj >= ts) | (ts >= SB_T), 1.0, 0.0).astype(BF16)
    qi = lax.broadcasted_iota(jnp.int32, (SB_T, SB_T), 0)
    ki = lax.broadcasted_iota(jnp.int32, (SB_T, SB_T), 1)
    diag = ki < qi

    def tile(s, start, carry, acc, mask=None):
        q = q_ref[s * SB_T:(s + 1) * SB_T, :]
        k = k_ref[pl.ds(start, SB_T), :]
        v = v_ref[pl.ds(start, SB_T), :]
        z = lax.dot_general(q, k, (((1,), (1,)), ((), ())), preferred_element_type=F32) * INV_SQRT_D
        lk = jnp.minimum(-z, 0.0) - jnp.log(1.0 + jnp.exp(-jnp.abs(z)))
        if mask is not None:
            lk = jnp.where(mask, lk, 0.0)
        hi = lk.astype(BF16)
        lo = (lk - hi.astype(F32)).astype(BF16)
        sums = jnp.dot(jnp.concatenate([hi, lo], axis=1), sum_rhs, preferred_element_type=F32)
        a = jnp.exp(z + sums[:, :SB_T] + carry)
        if mask is not None:
            a = jnp.where(mask, a, 0.0)
        acc = acc + jnp.dot(a.astype(BF16), v, preferred_element_type=F32)
        return carry + sums[:, SB_T:], acc

    zeros = jnp.zeros((SB_T, SB_T), F32)
    state = [tile(s, pl.multiple_of(r0 + s * SB_T, SB_T), zeros, zeros, diag) for s in range(SB_CHAINS)]

    def step(p, state):
        out = []
        for s in range(SB_CHAINS):
            carry, acc = state[s]
            start = r0 + (s - p) * SB_T
            carry = jnp.where(start >= 0, carry, SB_SKIP)
            start = pl.multiple_of(jnp.maximum(start, 0), SB_T)
            out.append(tile(s, start, carry, acc))
        return out

    for p in range(1, SB_STATIC_TILES + 1):
        state = step(p, state)

    def unfinished(p, state):
        live = state[0][0]
        for s in range(1, SB_CHAINS):
            live = jnp.maximum(live, state[s][0])
        has_keys = r0 + (SB_CHAINS - 1 - p) * SB_T >= 0
        return has_keys & (jnp.max(live) > SB_DONE)

    def cond(loop):
        return unfinished(*loop)

    def body(loop):
        p, state = loop
        return p + 1, step(p, state)

    _, state = lax.while_loop(cond, body, (jnp.int32(SB_STATIC_TILES + 1), state))
    for s in range(SB_CHAINS):
        o_ref[s * SB_T:(s + 1) * SB_T, :] = state[s][1].astype(o_ref.dtype)


def _stick_breaking_attention(sb, batch, seq):
    qkv_v = sb.reshape(batch, seq, 3 * SB_WIDTH)
    return pl.pallas_call(
        _sb_kernel,
        grid=(batch, SB_HEADS, seq // SB_TQ),
        in_specs=[
            pl.BlockSpec((None, SB_TQ, HEAD_DIM), lambda b, h, i: (b, i, h)),
            pl.BlockSpec((None, seq, HEAD_DIM), lambda b, h, i: (b, 0, SB_HEADS + h)),
            pl.BlockSpec((None, seq, HEAD_DIM), lambda b, h, i: (b, 0, 2 * SB_HEADS + h)),
        ],
        out_specs=pl.BlockSpec((None, SB_TQ, HEAD_DIM), lambda b, h, i: (b, i, h)),
        out_shape=jax.ShapeDtypeStruct((batch, seq, SB_WIDTH), BF16),
        compiler_params=_params(("parallel", "parallel", "arbitrary")),
        name="stick_breaking",
    )(qkv_v, qkv_v, qkv_v).reshape(batch * seq, SB_WIDTH)


MIX_TM = 256


def _mix_kernel(x_ref, o0_ref, o1_ref, o2_ref, l0_ref, l1_ref, l2_ref, ysb_ref, gate_ref, gb_ref,
                wud_ref, wus_ref, wo_ref, out_ref):
    heads = []
    for hh in range(HEADS_PER_GROUP):
        l0, l1, l2 = l0_ref[hh], l1_ref[hh], l2_ref[hh]
        m = jnp.maximum(jnp.maximum(l0, l1), l2)
        e0, e1, e2 = jnp.exp(l0 - m), jnp.exp(l1 - m), jnp.exp(l2 - m)
        y = (e0 * o0_ref[hh] + e1 * o1_ref[hh] + e2 * o2_ref[hh]) / (e0 + e1 + e2)
        heads.append(y.astype(BF16))
    y_dil = jnp.concatenate(heads, axis=1)
    up_dil = jnp.dot(y_dil, wud_ref[...], preferred_element_type=F32)
    up_sb = jnp.dot(ysb_ref[...], wus_ref[...], preferred_element_type=F32)
    gates = jax.nn.sigmoid(gate_ref[...].astype(F32) + gb_ref[...])
    mixed = gates[:, :D_MODEL] * up_dil + gates[:, D_MODEL:] * up_sb
    out_ref[...] = x_ref[...] + jnp.dot(mixed.astype(BF16), wo_ref[...], preferred_element_type=F32)


def _mix(x, outs, lds, y_sb, gates, gate_b, w_up_dil, w_up_sb, w_out):
    t = x.shape[0]
    rows = lambda w: pl.BlockSpec((MIX_TM, w), lambda i: (i, 0))
    full = lambda a: pl.BlockSpec(a.shape, lambda i: (0, 0))
    per_head = pl.BlockSpec((HEADS_PER_GROUP, MIX_TM, HEAD_DIM), lambda i: (0, i, 0))
    return pl.pallas_call(
        _mix_kernel,
        grid=(t // MIX_TM,),
        in_specs=[rows(D_MODEL)] + [per_head] * 6 + [rows(SB_WIDTH), rows(GATE_WIDTH),
                                                     full(gate_b), full(w_up_dil),
                                                     full(w_up_sb), full(w_out)],
        out_specs=rows(D_MODEL),
        out_shape=jax.ShapeDtypeStruct((t, D_MODEL), F32),
        compiler_params=_params(("parallel",)),
        name="mix",
    )(x, *outs, *lds, y_sb, gates, gate_b, w_up_dil, w_up_sb, w_out)


MLP_TM = 512
MLP_TF = 512


def _mlp_kernel(x_ref, g_ref, w1_ref, w2_ref, o_ref, h_ref):
    j = pl.program_id(1)

    @pl.when(j == 0)
    def _():
        x = x_ref[...]
        h_ref[...] = _rms_norm(x, g_ref[...]).astype(BF16)
        o_ref[...] = x

    u = jnp.maximum(jnp.dot(h_ref[...], w1_ref[...], preferred_element_type=F32), 0.0)
    o_ref[...] += jnp.dot((u * u).astype(BF16), w2_ref[...], preferred_element_type=F32)


def _mlp(x, g, w1, w2):
    t = x.shape[0]
    return pl.pallas_call(
        _mlp_kernel,
        grid=(t // MLP_TM, D_FF // MLP_TF),
        in_specs=[
            pl.BlockSpec((MLP_TM, D_MODEL), lambda i, j: (i, 0)),
            pl.BlockSpec((1, D_MODEL), lambda i, j: (0, 0)),
            pl.BlockSpec((D_MODEL, MLP_TF), lambda i, j: (0, j)),
            pl.BlockSpec((MLP_TF, D_MODEL), lambda i, j: (j, 0)),
        ],
        out_specs=pl.BlockSpec((MLP_TM, D_MODEL), lambda i, j: (i, 0)),
        out_shape=jax.ShapeDtypeStruct((t, D_MODEL), F32),
        scratch_shapes=[pltpu.VMEM((MLP_TM, D_MODEL), BF16)],
        compiler_params=_params(("parallel", "arbitrary")),
        name="mlp",
    )(x, g, w1, w2)


def _rope_table(seq):
    half = HEAD_DIM // 2
    inv_freq = ROPE_THETA ** (-jnp.arange(half, dtype=F32) / half)
    ang = jnp.arange(seq, dtype=F32)[:, None] * inv_freq[None, :]
    cos, sin = jnp.cos(ang), jnp.sin(ang)
    return jnp.concatenate([cos, cos, -sin, sin], axis=-1)


def kernel(x, norm1_g, w_in, q_norm_g, k_norm_g, w_up_dil, w_up_sb, gate_b, w_out, norm2_g, w_ff1, w_ff2):
    batch, seq, d = x.shape
    xf = x.reshape(batch * seq, d)
    cs = _rope_table(seq)
    for layer in range(DEPTH):
        *dil, sb, gates = _norm_proj(xf, norm1_g[layer][None], _reorder_w_in(w_in[layer]))
        outs, lds = [], []
        for group in range(N_GROUPS):
            o, ld = _dilated_attention(dil[group], cs, q_norm_g[layer, group][None],
                                       k_norm_g[layer, group][None], group, batch, seq)
            outs.append(o)
            lds.append(ld)
        y_sb = _stick_breaking_attention(sb, batch, seq)
        xf = _mix(xf, outs, lds, y_sb, gates, gate_b[layer].reshape(1, GATE_WIDTH),
                  w_up_dil[layer].astype(BF16), w_up_sb[layer].astype(BF16), w_out[layer].astype(BF16))
        xf = _mlp(xf, norm2_g[layer][None], w_ff1[layer].astype(BF16), w_ff2[layer].astype(BF16))
    return xf.reshape(batch, seq, d)
```

```python
import functools
import math

import jax
import jax.numpy as jnp
from jax import lax
from jax.experimental import pallas as pl
from jax.experimental.pallas import tpu as pltpu

D_MODEL = 2048
DEPTH = 2
HEAD_DIM = 128
DIL_GROUPS = ((128, 1), (512, 4), (2048, 16))
N_GROUPS = 3
HEADS_PER_GROUP = 4
GROUP_WIDTH = HEADS_PER_GROUP * HEAD_DIM
DIL_WIDTH = N_GROUPS * GROUP_WIDTH
SB_HEADS = 8
SB_WIDTH = SB_HEADS * HEAD_DIM
GATE_WIDTH = 2 * D_MODEL
D_FF = 4 * D_MODEL
ROPE_THETA = 10000.0
BLOCK = 128
EPS = 1e-6
INV_SQRT_D = 1.0 / math.sqrt(HEAD_DIM)

F32 = jnp.float32
BF16 = jnp.bfloat16

VMEM_LIMIT = 56 * 1024 * 1024


def _params(semantics):
    return pltpu.CompilerParams(dimension_semantics=semantics, vmem_limit_bytes=VMEM_LIMIT)


def _rms_norm(x, g):
    ms = jnp.mean(x * x, axis=-1, keepdims=True)
    return x * lax.rsqrt(ms + EPS) * g


PROJ_TM = 1024
PROJ_TN = 512
N_ROLES = 3
SB_BLOCK0 = N_GROUPS * N_ROLES
GATE_BLOCK0 = SB_BLOCK0 + 3 * SB_WIDTH // PROJ_TN
N_PROJ_BLOCKS = GATE_BLOCK0 + GATE_WIDTH // PROJ_TN


def _norm_proj_kernel(x_ref, g_ref, w_ref, d0_ref, d1_ref, d2_ref, sb_ref, gate_ref, h_ref, acc_ref):
    j = pl.program_id(1)

    @pl.when(j == 0)
    def _():
        h_ref[...] = _rms_norm(x_ref[...], g_ref[...]).astype(BF16)

    def project():
        return jnp.dot(h_ref[...], w_ref[...], preferred_element_type=F32)

    @pl.when(j < N_ROLES)
    def _():
        d0_ref[...] = project().astype(BF16)

    for group, d_ref in ((1, d1_ref), (2, d2_ref)):
        r = DIL_GROUPS[group][1]

        @pl.when((j >= group * N_ROLES) & (j < (group + 1) * N_ROLES))
        def _(r=r, d_ref=d_ref):
            acc = project()
            for lc in range(PROJ_TN // BLOCK):
                acc_ref[lc] = acc[:, lc * BLOCK:(lc + 1) * BLOCK]
            for c in range(r):
                for lc in range(PROJ_TN // BLOCK):
                    d_ref[:, c * PROJ_TN + lc * BLOCK: c * PROJ_TN + (lc + 1) * BLOCK] = (
                        acc_ref[lc, pl.ds(c, PROJ_TM // r, stride=r), :].astype(BF16))

    @pl.when((j >= SB_BLOCK0) & (j < GATE_BLOCK0))
    def _():
        sb_ref[...] = project().astype(BF16)

    @pl.when(j >= GATE_BLOCK0)
    def _():
        gate_ref[...] = project().astype(BF16)


def _held(j, first, count):
    return jnp.clip(j - first, 0, count - 1)


def _norm_proj(x, g, w):
    t = x.shape[0]

    def dil_spec(group):
        r = DIL_GROUPS[group][1]
        return pl.BlockSpec((None, PROJ_TM // r, r * PROJ_TN),
                            lambda i, j: (_held(j, group * N_ROLES, N_ROLES), i, 0))

    def dil_shape(group):
        r = DIL_GROUPS[group][1]
        return jax.ShapeDtypeStruct((N_ROLES, t // r, r * GROUP_WIDTH), BF16)

    return pl.pallas_call(
        _norm_proj_kernel,
        grid=(t // PROJ_TM, N_PROJ_BLOCKS),
        in_specs=[
            pl.BlockSpec((PROJ_TM, D_MODEL), lambda i, j: (i, 0)),
            pl.BlockSpec((1, D_MODEL), lambda i, j: (0, 0)),
            pl.BlockSpec((D_MODEL, PROJ_TN), lambda i, j: (0, j)),
        ],
        out_specs=[
            dil_spec(0), dil_spec(1), dil_spec(2),
            pl.BlockSpec((PROJ_TM, PROJ_TN), lambda i, j: (i, _held(j, SB_BLOCK0, GATE_BLOCK0 - SB_BLOCK0))),
            pl.BlockSpec((PROJ_TM, PROJ_TN), lambda i, j: (i, _held(j, GATE_BLOCK0, N_PROJ_BLOCKS - GATE_BLOCK0))),
        ],
        out_shape=[
            dil_shape(0), dil_shape(1), dil_shape(2),
            jax.ShapeDtypeStruct((t, 3 * SB_WIDTH), BF16),
            jax.ShapeDtypeStruct((t, GATE_WIDTH), BF16),
        ],
        scratch_shapes=[pltpu.VMEM((PROJ_TM, D_MODEL), BF16),
                        pltpu.VMEM((PROJ_TN // BLOCK, PROJ_TM, BLOCK), F32)],
        compiler_params=_params(("parallel", "arbitrary")),
        name="norm_proj",
    )(x, g, w)


def _reorder_w_in(w):
    cols = [w[:, role * DIL_WIDTH + group * GROUP_WIDTH: role * DIL_WIDTH + (group + 1) * GROUP_WIDTH]
            for group in range(N_GROUPS) for role in range(N_ROLES)]
    return jnp.concatenate(cols + [w[:, N_ROLES * DIL_WIDTH:]], axis=1).astype(BF16)


DIL_SHAPE = ((4, 1), (4, 1), (2, 2))


def _norm_rope(x, gain, cos, sin):
    y = _rms_norm(x, gain)
    return y * cos + pltpu.roll(y, HEAD_DIM // 2, 1) * sin


def _dil_kernel(q_ref, kc_ref, kp_ref, vc_ref, vp_ref, csc_ref, csp_ref, qg_ref, kg_ref,
                o_ref, ld_ref, *, r, n_sub, n_res):
    n = pl.program_id(1)
    cc = pl.program_id(2)
    qg = qg_ref[...]
    kg = kg_ref[...]

    row = lax.broadcasted_iota(jnp.int32, (BLOCK, 2 * BLOCK), 0)
    col = lax.broadcasted_iota(jnp.int32, (BLOCK, 2 * BLOCK), 1)
    off = BLOCK + row - col
    band = (off >= 0) & (off <= BLOCK)
    band_first = band & ((col >= BLOCK) | (n > 0))

    for cl in range(n_res):
        cos_sl = slice(cl * 2 * HEAD_DIM, cl * 2 * HEAD_DIM + HEAD_DIM)
        sin_sl = slice(cl * 2 * HEAD_DIM + HEAD_DIM, (cl + 1) * 2 * HEAD_DIM)
        cos_c = csc_ref[:, cos_sl]
        sin_c = csc_ref[:, sin_sl]
        cos_k = jnp.concatenate([csp_ref[:, cos_sl], cos_c], axis=0)
        sin_k = jnp.concatenate([csp_ref[:, sin_sl], sin_c], axis=0)
        for hh in range(HEADS_PER_GROUP):
            sl = slice(cl * GROUP_WIDTH + hh * HEAD_DIM, cl * GROUP_WIDTH + (hh + 1) * HEAD_DIM)
            q = _norm_rope(q_ref[:, sl].astype(F32), qg, cos_c, sin_c).astype(BF16)
            k_raw = jnp.concatenate([kp_ref[:, sl], kc_ref[:, sl]], axis=0).astype(F32)
            k = _norm_rope(k_raw, kg, cos_k, sin_k).astype(BF16)
            v = jnp.concatenate([vp_ref[:, sl], vc_ref[:, sl]], axis=0)
            for m in range(n_sub):
                rows = slice(m * BLOCK, (m + 1) * BLOCK)
                keys = slice(m * BLOCK, (m + 2) * BLOCK)
                s = lax.dot_general(q[rows], k[keys], (((1,), (1,)), ((), ())), preferred_element_type=F32)
                s = jnp.where(band_first if m == 0 else band, s * INV_SQRT_D, -jnp.inf)
                mx = jnp.max(s, axis=-1, keepdims=True)
                p = jnp.exp(s - mx)
                l = jnp.sum(p, axis=-1, keepdims=True)
                o = jnp.dot((p / l).astype(BF16), v[keys], preferred_element_type=F32)
                ld = jnp.broadcast_to(mx + jnp.log(l), (BLOCK, HEAD_DIM))
                if r == 1:
                    tok = rows
                else:
                    tok = pl.ds(m * BLOCK * r + cc * n_res + cl, BLOCK, stride=r)
                o_ref[hh, tok, :] = o
                ld_ref[hh, tok, :] = ld


def _dilated_attention(d, cs, q_gain, k_gain, group, batch, seq):
    _, r = DIL_GROUPS[group]
    n_sub, n_res = DIL_SHAPE[group]
    rows = n_sub * BLOCK
    nb = seq // r // rows
    cs_v = cs.reshape(seq // r, r * 2 * HEAD_DIM)
    width = n_res * GROUP_WIDTH

    def cur(role):
        return pl.BlockSpec((None, rows, width), lambda b, n, c: (role, b * nb + n, c))

    def prev(role):
        return pl.BlockSpec((None, BLOCK, width),
                            lambda b, n, c: (role, jnp.maximum((b * nb + n) * n_sub - 1, 0), c))

    out_spec = pl.BlockSpec((HEADS_PER_GROUP, rows * r, HEAD_DIM), lambda b, n, c: (0, b * nb + n, 0))
    out_sds = jax.ShapeDtypeStruct((HEADS_PER_GROUP, batch * seq, HEAD_DIM), F32)
    return pl.pallas_call(
        functools.partial(_dil_kernel, r=r, n_sub=n_sub, n_res=n_res),
        grid=(batch, nb, r // n_res),
        in_specs=[
            cur(0), cur(1), prev(1), cur(2), prev(2),
            pl.BlockSpec((rows, n_res * 2 * HEAD_DIM), lambda b, n, c: (n, c)),
            pl.BlockSpec((BLOCK, n_res * 2 * HEAD_DIM), lambda b, n, c: (jnp.maximum(n * n_sub - 1, 0), c)),
            pl.BlockSpec((1, HEAD_DIM), lambda b, n, c: (0, 0)),
            pl.BlockSpec((1, HEAD_DIM), lambda b, n, c: (0, 0)),
        ],
        out_specs=[out_spec, out_spec],
        out_shape=[out_sds, out_sds],
        compiler_params=_params(("parallel", "parallel", "arbitrary")),
        name=f"dilated_r{r}",
    )(d, d, d, d, d, cs_v, cs_v, q_gain, k_gain)


SB_T = 128
SB_CHAINS = 4
SB_TQ = SB_CHAINS * SB_T
SB_STATIC_TILES = 2
SB_DONE = -104.0
SB_SKIP = -1e30


def _sb_kernel(q_ref, k_ref, v_ref, o_ref):
    i = pl.program_id(2)
    r0 = i * SB_TQ
    tj = lax.broadcasted_iota(jnp.int32, (2 * SB_T, 2 * SB_T), 0) % SB_T
    ts = lax.broadcasted_iota(jnp.int32, (2 * SB_T, 2 * SB_T), 1)
    sum_rhs = jnp.where((tj >= ts) | (ts >= SB_T), 1.0, 0.0).astype(BF16)
    qi = lax.broadcasted_iota(jnp.int32, (SB_T, SB_T), 0)
    ki = lax.broadcasted_iota(jnp.int32, (SB_T, SB_T), 1)
    diag = ki < qi

    def tile(s, start, carry, acc, mask=None):
        q = q_ref[s * SB_T:(s + 1) * SB_T, :]
        k = k_ref[pl.ds(start, SB_T), :]
        v = v_ref[pl.ds(start, SB_T), :]
        z = lax.dot_general(q, k, (((1,), (1,)), ((), ())), preferred_element_type=F32) * INV_SQRT_D
        lk = jnp.minimum(-z, 0.0) - jnp.log(1.0 + jnp.exp(-jnp.abs(z)))
        if mask is not None:
            lk = jnp.where(mask, lk, 0.0)
        hi = lk.astype(BF16)
        lo = (lk - hi.astype(F32)).astype(BF16)
        sums = jnp.dot(jnp.concatenate([hi, lo], axis=1), sum_rhs, preferred_element_type=F32)
        a = jnp.exp(z + sums[:, :SB_T] + carry)
        if mask is not None:
            a = jnp.where(mask, a, 0.0)
        acc = acc + jnp.dot(a.astype(BF16), v, preferred_element_type=F32)
        return carry + sums[:, SB_T:], acc

    zeros = jnp.zeros((SB_T, SB_T), F32)
    state = [tile(s, pl.multiple_of(r0 + s * SB_T, SB_T), zeros, zeros, diag) for s in range(SB_CHAINS)]

    def step(p, state):
        out = []
        for s in range(SB_CHAINS):
            carry, acc = state[s]
            start = r0 + (s - p) * SB_T
            carry = jnp.where(start >= 0, carry, SB_SKIP)
            start = pl.multiple_of(jnp.maximum(start, 0), SB_T)
            out.append(tile(s, start, carry, acc))
        return out

    for p in range(1, SB_STATIC_TILES + 1):
        state = step(p, state)

    def unfinished(p, state):
        live = state[0][0]
        for s in range(1, SB_CHAINS):
            live = jnp.maximum(live, state[s][0])
        has_keys = r0 + (SB_CHAINS - 1 - p) * SB_T >= 0
        return has_keys & (jnp.max(live) > SB_DONE)

    def cond(loop):
        return unfinished(*loop)

    def body(loop):
        p, state = loop
        return p + 1, step(p, state)

    _, state = lax.while_loop(cond, body, (jnp.int32(SB_STATIC_TILES + 1), state))
    for s in range(SB_CHAINS):
        o_ref[s * SB_T:(s + 1) * SB_T, :] = state[s][1].astype(o_ref.dtype)


def _stick_breaking_attention(sb, batch, seq):
    qkv_v = sb.reshape(batch, seq, 3 * SB_WIDTH)
    return pl.pallas_call(
        _sb_kernel,
        grid=(batch, SB_HEADS, seq // SB_TQ),
        in_specs=[
            pl.BlockSpec((None, SB_TQ, HEAD_DIM), lambda b, h, i: (b, i, h)),
            pl.BlockSpec((None, seq, HEAD_DIM), lambda b, h, i: (b, 0, SB_HEADS + h)),
            pl.BlockSpec((None, seq, HEAD_DIM), lambda b, h, i: (b, 0, 2 * SB_HEADS + h)),
        ],
        out_specs=pl.BlockSpec((None, SB_TQ, HEAD_DIM), lambda b, h, i: (b, i, h)),
        out_shape=jax.ShapeDtypeStruct((batch, seq, SB_WIDTH), BF16),
        compiler_params=_params(("parallel", "parallel", "arbitrary")),
        name="stick_breaking",
    )(qkv_v, qkv_v, qkv_v).reshape(batch * seq, SB_WIDTH)


MIX_TM = 256


def _mix_kernel(x_ref, o0_ref, o1_ref, o2_ref, l0_ref, l1_ref, l2_ref, ysb_ref, gate_ref, gb_ref,
                wud_ref, wus_ref, wo_ref, out_ref):
    heads = []
    for hh in range(HEADS_PER_GROUP):
        l0, l1, l2 = l0_ref[hh], l1_ref[hh], l2_ref[hh]
        m = jnp.maximum(jnp.maximum(l0, l1), l2)
        e0, e1, e2 = jnp.exp(l0 - m), jnp.exp(l1 - m), jnp.exp(l2 - m)
        y = (e0 * o0_ref[hh] + e1 * o1_ref[hh] + e2 * o2_ref[hh]) / (e0 + e1 + e2)
        heads.append(y.astype(BF16))
    y_dil = jnp.concatenate(heads, axis=1)
    up_dil = jnp.dot(y_dil, wud_ref[...], preferred_element_type=F32)
    up_sb = jnp.dot(ysb_ref[...], wus_ref[...], preferred_element_type=F32)
    gates = jax.nn.sigmoid(gate_ref[...].astype(F32) + gb_ref[...])
    mixed = gates[:, :D_MODEL] * up_dil + gates[:, D_MODEL:] * up_sb
    out_ref[...] = x_ref[...] + jnp.dot(mixed.astype(BF16), wo_ref[...], preferred_element_type=F32)


def _mix(x, outs, lds, y_sb, gates, gate_b, w_up_dil, w_up_sb, w_out):
    t = x.shape[0]
    rows = lambda w: pl.BlockSpec((MIX_TM, w), lambda i: (i, 0))
    full = lambda a: pl.BlockSpec(a.shape, lambda i: (0, 0))
    per_head = pl.BlockSpec((HEADS_PER_GROUP, MIX_TM, HEAD_DIM), lambda i: (0, i, 0))
    return pl.pallas_call(
        _mix_kernel,
        grid=(t // MIX_TM,),
        in_specs=[rows(D_MODEL)] + [per_head] * 6 + [rows(SB_WIDTH), rows(GATE_WIDTH),
                                                     full(gate_b), full(w_up_dil),
                                                     full(w_up_sb), full(w_out)],
        out_specs=rows(D_MODEL),
        out_shape=jax.ShapeDtypeStruct((t, D_MODEL), F32),
        compiler_params=_params(("parallel",)),
        name="mix",
    )(x, *outs, *lds, y_sb, gates, gate_b, w_up_dil, w_up_sb, w_out)


MLP_TM = 1024
MLP_TF = 512


def _mlp_kernel(x_ref, g_ref, w1_ref, w2_ref, o_ref, h_ref):
    j = pl.program_id(1)

    @pl.when(j == 0)
    def _():
        x = x_ref[...]
        h_ref[...] = _rms_norm(x, g_ref[...]).astype(BF16)
        o_ref[...] = x

    u = jnp.maximum(jnp.dot(h_ref[...], w1_ref[...], preferred_element_type=F32), 0.0)
    o_ref[...] += jnp.dot((u * u).astype(BF16), w2_ref[...], preferred_element_type=F32)


def _mlp(x, g, w1, w2):
    t = x.shape[0]
    return pl.pallas_call(
        _mlp_kernel,
        grid=(t // MLP_TM, D_FF // MLP_TF),
        in_specs=[
            pl.BlockSpec((MLP_TM, D_MODEL), lambda i, j: (i, 0)),
            pl.BlockSpec((1, D_MODEL), lambda i, j: (0, 0)),
            pl.BlockSpec((D_MODEL, MLP_TF), lambda i, j: (0, j)),
            pl.BlockSpec((MLP_TF, D_MODEL), lambda i, j: (j, 0)),
        ],
        out_specs=pl.BlockSpec((MLP_TM, D_MODEL), lambda i, j: (i, 0)),
        out_shape=jax.ShapeDtypeStruct((t, D_MODEL), F32),
        scratch_shapes=[pltpu.VMEM((MLP_TM, D_MODEL), BF16)],
        compiler_params=_params(("parallel", "arbitrary")),
        name="mlp",
    )(x, g, w1, w2)


def _rope_table(seq):
    half = HEAD_DIM // 2
    inv_freq = ROPE_THETA ** (-jnp.arange(half, dtype=F32) / half)
    ang = jnp.arange(seq, dtype=F32)[:, None] * inv_freq[None, :]
    cos, sin = jnp.cos(ang), jnp.sin(ang)
    return jnp.concatenate([cos, cos, -sin, sin], axis=-1)


def kernel(x, norm1_g, w_in, q_norm_g, k_norm_g, w_up_dil, w_up_sb, gate_b, w_out, norm2_g, w_ff1, w_ff2):
    batch, seq, d = x.shape
    xf = x.reshape(batch * seq, d)
    cs = _rope_table(seq)
    for layer in range(DEPTH):
        *dil, sb, gates = _norm_proj(xf, norm1_g[layer][None], _reorder_w_in(w_in[layer]))
        outs, lds = [], []
        for group in range(N_GROUPS):
            o, ld = _dilated_attention(dil[group], cs, q_norm_g[layer, group][None],
                                       k_norm_g[layer, group][None], group, batch, seq)
            outs.append(o)
            lds.append(ld)
        y_sb = _stick_breaking_attention(sb, batch, seq)
        xf = _mix(xf, outs, lds, y_sb, gates, gate_b[layer].reshape(1, GATE_WIDTH),
                  w_up_dil[layer].astype(BF16), w_up_sb[layer].astype(BF16), w_out[layer].astype(BF16))
        xf = _mlp(xf, norm2_g[layer][None], w_ff1[layer].astype(BF16), w_ff2[layer].astype(BF16))
    return xf.reshape(batch, seq, d)
```

```python
import functools
import math

import jax
import jax.numpy as jnp
from jax import lax
from jax.experimental import pallas as pl
from jax.experimental.pallas import tpu as pltpu

D_MODEL = 2048
DEPTH = 2
HEAD_DIM = 128
DIL_GROUPS = ((128, 1), (512, 4), (2048, 16))
N_GROUPS = 3
HEADS_PER_GROUP = 4
GROUP_WIDTH = HEADS_PER_GROUP * HEAD_DIM
DIL_WIDTH = N_GROUPS * GROUP_WIDTH
SB_HEADS = 8
SB_WIDTH = SB_HEADS * HEAD_DIM
GATE_WIDTH = 2 * D_MODEL
D_FF = 4 * D_MODEL
ROPE_THETA = 10000.0
BLOCK = 128
EPS = 1e-6
INV_SQRT_D = 1.0 / math.sqrt(HEAD_DIM)

F32 = jnp.float32
BF16 = jnp.bfloat16

VMEM_LIMIT = 56 * 1024 * 1024


def _params(semantics):
    return pltpu.CompilerParams(dimension_semantics=semantics, vmem_limit_bytes=VMEM_LIMIT)


def _rms_norm(x, g):
    ms = jnp.mean(x * x, axis=-1, keepdims=True)
    return x * lax.rsqrt(ms + EPS) * g


PROJ_TM = 1024
PROJ_TN = 512
N_ROLES = 3
SB_BLOCK0 = N_GROUPS * N_ROLES
GATE_BLOCK0 = SB_BLOCK0 + 3 * SB_WIDTH // PROJ_TN
N_PROJ_BLOCKS = GATE_BLOCK0 + GATE_WIDTH // PROJ_TN


def _norm_proj_kernel(x_ref, g_ref, w_ref, d0_ref, d1_ref, d2_ref, sb_ref, gate_ref, h_ref, acc_ref):
    j = pl.program_id(1)

    @pl.when(j == 0)
    def _():
        h_ref[...] = _rms_norm(x_ref[...], g_ref[...]).astype(BF16)

    def project():
        return jnp.dot(h_ref[...], w_ref[...], preferred_element_type=F32)

    @pl.when(j < N_ROLES)
    def _():
        d0_ref[...] = project().astype(BF16)

    for group, d_ref in ((1, d1_ref), (2, d2_ref)):
        r = DIL_GROUPS[group][1]

        @pl.when((j >= group * N_ROLES) & (j < (group + 1) * N_ROLES))
        def _(r=r, d_ref=d_ref):
            acc = project()
            for lc in range(PROJ_TN // BLOCK):
                acc_ref[lc] = acc[:, lc * BLOCK:(lc + 1) * BLOCK]
            for c in range(r):
                for lc in range(PROJ_TN // BLOCK):
                    d_ref[:, c * PROJ_TN + lc * BLOCK: c * PROJ_TN + (lc + 1) * BLOCK] = (
                        acc_ref[lc, pl.ds(c, PROJ_TM // r, stride=r), :].astype(BF16))

    @pl.when((j >= SB_BLOCK0) & (j < GATE_BLOCK0))
    def _():
        sb_ref[...] = project().astype(BF16)

    @pl.when(j >= GATE_BLOCK0)
    def _():
        gate_ref[...] = project().astype(BF16)


def _held(j, first, count):
    return jnp.clip(j - first, 0, count - 1)


def _norm_proj(x, g, w):
    t = x.shape[0]

    def dil_spec(group):
        r = DIL_GROUPS[group][1]
        return pl.BlockSpec((None, PROJ_TM // r, r * PROJ_TN),
                            lambda i, j: (_held(j, group * N_ROLES, N_ROLES), i, 0))

    def dil_shape(group):
        r = DIL_GROUPS[group][1]
        return jax.ShapeDtypeStruct((N_ROLES, t // r, r * GROUP_WIDTH), BF16)

    return pl.pallas_call(
        _norm_proj_kernel,
        grid=(t // PROJ_TM, N_PROJ_BLOCKS),
        in_specs=[
            pl.BlockSpec((PROJ_TM, D_MODEL), lambda i, j: (i, 0)),
            pl.BlockSpec((1, D_MODEL), lambda i, j: (0, 0)),
            pl.BlockSpec((None, D_MODEL, PROJ_TN), lambda i, j: (j, 0, 0)),
        ],
        out_specs=[
            dil_spec(0), dil_spec(1), dil_spec(2),
            pl.BlockSpec((None, PROJ_TM, PROJ_TN),
                         lambda i, j: (_held(j, SB_BLOCK0, GATE_BLOCK0 - SB_BLOCK0), i, 0)),
            pl.BlockSpec((None, PROJ_TM, PROJ_TN),
                         lambda i, j: (_held(j, GATE_BLOCK0, N_PROJ_BLOCKS - GATE_BLOCK0), i, 0)),
        ],
        out_shape=[
            dil_shape(0), dil_shape(1), dil_shape(2),
            jax.ShapeDtypeStruct((GATE_BLOCK0 - SB_BLOCK0, t, PROJ_TN), BF16),
            jax.ShapeDtypeStruct((N_PROJ_BLOCKS - GATE_BLOCK0, t, PROJ_TN), BF16),
        ],
        scratch_shapes=[pltpu.VMEM((PROJ_TM, D_MODEL), BF16),
                        pltpu.VMEM((PROJ_TN // BLOCK, PROJ_TM, BLOCK), F32)],
        compiler_params=_params(("parallel", "arbitrary")),
        name="norm_proj",
    )(x, g, w)


def _reorder_w_in(w):
    cols = [w[:, role * DIL_WIDTH + group * GROUP_WIDTH: role * DIL_WIDTH + (group + 1) * GROUP_WIDTH]
            for group in range(N_GROUPS) for role in range(N_ROLES)]
    return jnp.concatenate([jnp.stack(cols), _column_blocks(w[:, N_ROLES * DIL_WIDTH:], PROJ_TN)],
                           axis=0).astype(BF16)


def _column_blocks(w, width):
    k, n = w.shape
    return w.reshape(k, n // width, width).transpose(1, 0, 2)


DIL_SHAPE = ((4, 1), (4, 1), (2, 2))


def _norm_rope(x, gain, cos, sin):
    y = _rms_norm(x, gain)
    return y * cos + pltpu.roll(y, HEAD_DIM // 2, 1) * sin


def _dil_kernel(q_ref, kc_ref, kp_ref, vc_ref, vp_ref, csc_ref, csp_ref, qg_ref, kg_ref,
                o_ref, ld_ref, *, r, n_sub, n_res):
    n = pl.program_id(1)
    cc = pl.program_id(2)
    qg = qg_ref[...]
    kg = kg_ref[...]

    row = lax.broadcasted_iota(jnp.int32, (BLOCK, 2 * BLOCK), 0)
    col = lax.broadcasted_iota(jnp.int32, (BLOCK, 2 * BLOCK), 1)
    off = BLOCK + row - col
    band = (off >= 0) & (off <= BLOCK)
    band_first = band & ((col >= BLOCK) | (n > 0))

    for cl in range(n_res):
        cos_sl = slice(cl * 2 * HEAD_DIM, cl * 2 * HEAD_DIM + HEAD_DIM)
        sin_sl = slice(cl * 2 * HEAD_DIM + HEAD_DIM, (cl + 1) * 2 * HEAD_DIM)
        cos_c = csc_ref[:, cos_sl]
        sin_c = csc_ref[:, sin_sl]
        cos_k = jnp.concatenate([csp_ref[:, cos_sl], cos_c], axis=0)
        sin_k = jnp.concatenate([csp_ref[:, sin_sl], sin_c], axis=0)
        for hh in range(HEADS_PER_GROUP):
            sl = slice(cl * GROUP_WIDTH + hh * HEAD_DIM, cl * GROUP_WIDTH + (hh + 1) * HEAD_DIM)
            q = _norm_rope(q_ref[:, sl].astype(F32), qg, cos_c, sin_c).astype(BF16)
            k_raw = jnp.concatenate([kp_ref[:, sl], kc_ref[:, sl]], axis=0).astype(F32)
            k = _norm_rope(k_raw, kg, cos_k, sin_k).astype(BF16)
            v = jnp.concatenate([vp_ref[:, sl], vc_ref[:, sl]], axis=0)
            for m in range(n_sub):
                rows = slice(m * BLOCK, (m + 1) * BLOCK)
                keys = slice(m * BLOCK, (m + 2) * BLOCK)
                s = lax.dot_general(q[rows], k[keys], (((1,), (1,)), ((), ())), preferred_element_type=F32)
                s = jnp.where(band_first if m == 0 else band, s * INV_SQRT_D, -jnp.inf)
                mx = jnp.max(s, axis=-1, keepdims=True)
                p = jnp.exp(s - mx)
                l = jnp.sum(p, axis=-1, keepdims=True)
                o = jnp.dot((p / l).astype(BF16), v[keys], preferred_element_type=F32)
                ld = jnp.broadcast_to(mx + jnp.log(l), (BLOCK, HEAD_DIM))
                if r == 1:
                    tok = rows
                else:
                    tok = pl.ds(m * BLOCK * r + cc * n_res + cl, BLOCK, stride=r)
                o_ref[hh, tok, :] = o
                ld_ref[hh, tok, :] = ld


def _dilated_attention(d, cs, q_gain, k_gain, group, batch, seq):
    _, r = DIL_GROUPS[group]
    n_sub, n_res = DIL_SHAPE[group]
    rows = n_sub * BLOCK
    nb = seq // r // rows
    cs_v = cs.reshape(seq // r, r * 2 * HEAD_DIM)
    width = n_res * GROUP_WIDTH

    def cur(role):
        return pl.BlockSpec((None, rows, width), lambda b, n, c: (role, b * nb + n, c))

    def prev(role):
        return pl.BlockSpec((None, BLOCK, width),
                            lambda b, n, c: (role, jnp.maximum((b * nb + n) * n_sub - 1, 0), c))

    out_spec = pl.BlockSpec((HEADS_PER_GROUP, rows * r, HEAD_DIM), lambda b, n, c: (0, b * nb + n, 0))
    out_sds = jax.ShapeDtypeStruct((HEADS_PER_GROUP, batch * seq, HEAD_DIM), F32)
    return pl.pallas_call(
        functools.partial(_dil_kernel, r=r, n_sub=n_sub, n_res=n_res),
        grid=(batch, nb, r // n_res),
        in_specs=[
            cur(0), cur(1), prev(1), cur(2), prev(2),
            pl.BlockSpec((rows, n_res * 2 * HEAD_DIM), lambda b, n, c: (n, c)),
            pl.BlockSpec((BLOCK, n_res * 2 * HEAD_DIM), lambda b, n, c: (jnp.maximum(n * n_sub - 1, 0), c)),
            pl.BlockSpec((1, HEAD_DIM), lambda b, n, c: (0, 0)),
            pl.BlockSpec((1, HEAD_DIM), lambda b, n, c: (0, 0)),
        ],
        out_specs=[out_spec, out_spec],
        out_shape=[out_sds, out_sds],
        compiler_params=_params(("parallel", "parallel", "arbitrary")),
        name=f"dilated_r{r}",
    )(d, d, d, d, d, cs_v, cs_v, q_gain, k_gain)


SB_T = 128
SB_CHAINS = 4
SB_TQ = SB_CHAINS * SB_T
SB_STATIC_TILES = 2
SB_DONE = -104.0
SB_SKIP = -1e30


def _sb_kernel(q_ref, k_ref, v_ref, o_ref):
    i = pl.program_id(2)
    r0 = i * SB_TQ
    tj = lax.broadcasted_iota(jnp.int32, (2 * SB_T, 2 * SB_T), 0) % SB_T
    ts = lax.broadcasted_iota(jnp.int32, (2 * SB_T, 2 * SB_T), 1)
    sum_rhs = jnp.where((tj >= ts) | (ts >= SB_T), 1.0, 0.0).astype(BF16)
    qi = lax.broadcasted_iota(jnp.int32, (SB_T, SB_T), 0)
    ki = lax.broadcasted_iota(jnp.int32, (SB_T, SB_T), 1)
    diag = ki < qi

    def tile(s, start, carry, acc, mask=None):
        q = q_ref[s * SB_T:(s + 1) * SB_T, :]
        k = k_ref[pl.ds(start, SB_T), :]
        v = v_ref[pl.ds(start, SB_T), :]
        z = lax.dot_general(q, k, (((1,), (1,)), ((), ())), preferred_element_type=F32) * INV_SQRT_D
        lk = jnp.minimum(-z, 0.0) - jnp.log(1.0 + jnp.exp(-jnp.abs(z)))
        if mask is not None:
            lk = jnp.where(mask, lk, 0.0)
        hi = lk.astype(BF16)
        lo = (lk - hi.astype(F32)).astype(BF16)
        sums = jnp.dot(jnp.concatenate([hi, lo], axis=1), sum_rhs, preferred_element_type=F32)
        a = jnp.exp(z + sums[:, :SB_T] + carry)
        if mask is not None:
            a = jnp.where(mask, a, 0.0)
        acc = acc + jnp.dot(a.astype(BF16), v, preferred_element_type=F32)
        return carry + sums[:, SB_T:], acc

    zeros = jnp.zeros((SB_T, SB_T), F32)
    state = [tile(s, pl.multiple_of(r0 + s * SB_T, SB_T), zeros, zeros, diag) for s in range(SB_CHAINS)]

    def step(p, state):
        out = []
        for s in range(SB_CHAINS):
            carry, acc = state[s]
            start = r0 + (s - p) * SB_T
            carry = jnp.where(start >= 0, carry, SB_SKIP)
            start = pl.multiple_of(jnp.maximum(start, 0), SB_T)
            out.append(tile(s, start, carry, acc))
        return out

    for p in range(1, SB_STATIC_TILES + 1):
        state = step(p, state)

    def unfinished(p, state):
        live = state[0][0]
        for s in range(1, SB_CHAINS):
            live = jnp.maximum(live, state[s][0])
        has_keys = r0 + (SB_CHAINS - 1 - p) * SB_T >= 0
        return has_keys & (jnp.max(live) > SB_DONE)

    def cond(loop):
        return unfinished(*loop)

    def body(loop):
        p, state = loop
        return p + 1, step(p, state)

    _, state = lax.while_loop(cond, body, (jnp.int32(SB_STATIC_TILES + 1), state))
    for s in range(SB_CHAINS):
        o_ref[s * SB_T:(s + 1) * SB_T, :] = state[s][1].astype(o_ref.dtype)


def _stick_breaking_attention(sb, batch, seq):
    per_block = PROJ_TN // HEAD_DIM
    qkv_v = sb.reshape(sb.shape[0], batch, seq, PROJ_TN)

    def head_spec(rows, first_head, whole_sequence):
        def index(b, h, i):
            col = first_head + h
            return col // per_block, b, 0 if whole_sequence else i, col % per_block
        return pl.BlockSpec((None, None, rows, HEAD_DIM), index)

    return pl.pallas_call(
        _sb_kernel,
        grid=(batch, SB_HEADS, seq // SB_TQ),
        in_specs=[
            head_spec(SB_TQ, 0, False),
            head_spec(seq, SB_HEADS, True),
            head_spec(seq, 2 * SB_HEADS, True),
        ],
        out_specs=pl.BlockSpec((None, SB_TQ, HEAD_DIM), lambda b, h, i: (b, i, h)),
        out_shape=jax.ShapeDtypeStruct((batch, seq, SB_WIDTH), BF16),
        compiler_params=_params(("parallel", "parallel", "arbitrary")),
        name="stick_breaking",
    )(qkv_v, qkv_v, qkv_v).reshape(batch * seq, SB_WIDTH)


MIX_TM = 256


def _mix_kernel(x_ref, o0_ref, o1_ref, o2_ref, l0_ref, l1_ref, l2_ref, ysb_ref, gate_ref, gb_ref,
                wud_ref, wus_ref, wo_ref, out_ref):
    heads = []
    for hh in range(HEADS_PER_GROUP):
        l0, l1, l2 = l0_ref[hh], l1_ref[hh], l2_ref[hh]
        m = jnp.maximum(jnp.maximum(l0, l1), l2)
        e0, e1, e2 = jnp.exp(l0 - m), jnp.exp(l1 - m), jnp.exp(l2 - m)
        y = (e0 * o0_ref[hh] + e1 * o1_ref[hh] + e2 * o2_ref[hh]) / (e0 + e1 + e2)
        heads.append(y.astype(BF16))
    y_dil = jnp.concatenate(heads, axis=1)
    up_dil = jnp.dot(y_dil, wud_ref[...], preferred_element_type=F32)
    up_sb = jnp.dot(ysb_ref[...], wus_ref[...], preferred_element_type=F32)
    gate_pre = jnp.concatenate([gate_ref[c] for c in range(GATE_WIDTH // PROJ_TN)], axis=1)
    gates = jax.nn.sigmoid(gate_pre.astype(F32) + gb_ref[...])
    mixed = gates[:, :D_MODEL] * up_dil + gates[:, D_MODEL:] * up_sb
    out_ref[...] = x_ref[...] + jnp.dot(mixed.astype(BF16), wo_ref[...], preferred_element_type=F32)


def _mix(x, outs, lds, y_sb, gates, gate_b, w_up_dil, w_up_sb, w_out):
    t = x.shape[0]
    rows = lambda w: pl.BlockSpec((MIX_TM, w), lambda i: (i, 0))
    full = lambda a: pl.BlockSpec(a.shape, lambda i: (0, 0))
    per_head = pl.BlockSpec((HEADS_PER_GROUP, MIX_TM, HEAD_DIM), lambda i: (0, i, 0))
    gate_blocks = pl.BlockSpec((GATE_WIDTH // PROJ_TN, MIX_TM, PROJ_TN), lambda i: (0, i, 0))
    return pl.pallas_call(
        _mix_kernel,
        grid=(t // MIX_TM,),
        in_specs=[rows(D_MODEL)] + [per_head] * 6 + [rows(SB_WIDTH), gate_blocks,
                                                     full(gate_b), full(w_up_dil),
                                                     full(w_up_sb), full(w_out)],
        out_specs=rows(D_MODEL),
        out_shape=jax.ShapeDtypeStruct((t, D_MODEL), F32),
        compiler_params=_params(("parallel",)),
        name="mix",
    )(x, *outs, *lds, y_sb, gates, gate_b, w_up_dil, w_up_sb, w_out)


MLP_TM = 1024
MLP_TF = 512


def _mlp_kernel(x_ref, g_ref, w1_ref, w2_ref, o_ref, h_ref):
    j = pl.program_id(1)

    @pl.when(j == 0)
    def _():
        x = x_ref[...]
        h_ref[...] = _rms_norm(x, g_ref[...]).astype(BF16)
        o_ref[...] = x

    u = jnp.maximum(jnp.dot(h_ref[...], w1_ref[...], preferred_element_type=F32), 0.0)
    o_ref[...] += jnp.dot((u * u).astype(BF16), w2_ref[...], preferred_element_type=F32)


def _mlp(x, g, w1, w2):
    t = x.shape[0]
    return pl.pallas_call(
        _mlp_kernel,
        grid=(t // MLP_TM, D_FF // MLP_TF),
        in_specs=[
            pl.BlockSpec((MLP_TM, D_MODEL), lambda i, j: (i, 0)),
            pl.BlockSpec((1, D_MODEL), lambda i, j: (0, 0)),
            pl.BlockSpec((None, D_MODEL, MLP_TF), lambda i, j: (j, 0, 0)),
            pl.BlockSpec((MLP_TF, D_MODEL), lambda i, j: (j, 0)),
        ],
        out_specs=pl.BlockSpec((MLP_TM, D_MODEL), lambda i, j: (i, 0)),
        out_shape=jax.ShapeDtypeStruct((t, D_MODEL), F32),
        scratch_shapes=[pltpu.VMEM((MLP_TM, D_MODEL), BF16)],
        compiler_params=_params(("parallel", "arbitrary")),
        name="mlp",
    )(x, g, w1, w2)


def _rope_table(seq):
    half = HEAD_DIM // 2
    inv_freq = ROPE_THETA ** (-jnp.arange(half, dtype=F32) / half)
    ang = jnp.arange(seq, dtype=F32)[:, None] * inv_freq[None, :]
    cos, sin = jnp.cos(ang), jnp.sin(ang)
    return jnp.concatenate([cos, cos, -sin, sin], axis=-1)


def kernel(x, norm1_g, w_in, q_norm_g, k_norm_g, w_up_dil, w_up_sb, gate_b, w_out, norm2_g, w_ff1, w_ff2):
    batch, seq, d = x.shape
    xf = x.reshape(batch * seq, d)
    cs = _rope_table(seq)
    for layer in range(DEPTH):
        *dil, sb, gates = _norm_proj(xf, norm1_g[layer][None], _reorder_w_in(w_in[layer]))
        outs, lds = [], []
        for group in range(N_GROUPS):
            o, ld = _dilated_attention(dil[group], cs, q_norm_g[layer, group][None],
                                       k_norm_g[layer, group][None], group, batch, seq)
            outs.append(o)
            lds.append(ld)
        y_sb = _stick_breaking_attention(sb, batch, seq)
        xf = _mix(xf, outs, lds, y_sb, gates, gate_b[layer].reshape(1, GATE_WIDTH),
                  w_up_dil[layer].astype(BF16), w_up_sb[layer].astype(BF16), w_out[layer].astype(BF16))
        xf = _mlp(xf, norm2_g[layer][None], _column_blocks(w_ff1[layer], MLP_TF).astype(BF16),
                  w_ff2[layer].astype(BF16))
    return xf.reshape(batch, seq, d)
```

```python
import functools
import math

import jax
import jax.numpy as jnp
from jax import lax
from jax.experimental import pallas as pl
from jax.experimental.pallas import tpu as pltpu

D_MODEL = 2048
DEPTH = 2
HEAD_DIM = 128
DIL_GROUPS = ((128, 1), (512, 4), (2048, 16))
N_GROUPS = 3
HEADS_PER_GROUP = 4
GROUP_WIDTH = HEADS_PER_GROUP * HEAD_DIM
DIL_WIDTH = N_GROUPS * GROUP_WIDTH
SB_HEADS = 8
SB_WIDTH = SB_HEADS * HEAD_DIM
GATE_WIDTH = 2 * D_MODEL
D_FF = 4 * D_MODEL
ROPE_THETA = 10000.0
BLOCK = 128
EPS = 1e-6
INV_SQRT_D = 1.0 / math.sqrt(HEAD_DIM)

F32 = jnp.float32
BF16 = jnp.bfloat16

VMEM_LIMIT = 56 * 1024 * 1024


def _params(semantics):
    return pltpu.CompilerParams(dimension_semantics=semantics, vmem_limit_bytes=VMEM_LIMIT)


def _rms_norm(x, g):
    ms = jnp.mean(x * x, axis=-1, keepdims=True)
    return x * lax.rsqrt(ms + EPS) * g


PROJ_TM = 1024
PROJ_TN = 512
N_ROLES = 3
PROJ_TN_WIDE = 1024
SB_BLOCK0 = N_GROUPS * N_ROLES
GATE_BLOCK0 = SB_BLOCK0 + 3 * SB_WIDTH // PROJ_TN_WIDE
N_PROJ_BLOCKS = GATE_BLOCK0 + GATE_WIDTH // PROJ_TN_WIDE


def _norm_proj_kernel(x_ref, g_ref, wd_ref, ww_ref, d0_ref, d1_ref, d2_ref, sb_ref, gate_ref,
                      h_ref, acc_ref):
    j = pl.program_id(1)

    @pl.when(j == 0)
    def _():
        h_ref[...] = _rms_norm(x_ref[...], g_ref[...]).astype(BF16)

    def project(w_ref=wd_ref):
        return jnp.dot(h_ref[...], w_ref[...], preferred_element_type=F32)

    @pl.when(j < N_ROLES)
    def _():
        d0_ref[...] = project().astype(BF16)

    for group, d_ref in ((1, d1_ref), (2, d2_ref)):
        r = DIL_GROUPS[group][1]

        @pl.when((j >= group * N_ROLES) & (j < (group + 1) * N_ROLES))
        def _(r=r, d_ref=d_ref):
            acc = project()
            for lc in range(PROJ_TN // BLOCK):
                acc_ref[lc] = acc[:, lc * BLOCK:(lc + 1) * BLOCK]
            for c in range(r):
                for lc in range(PROJ_TN // BLOCK):
                    d_ref[:, c * PROJ_TN + lc * BLOCK: c * PROJ_TN + (lc + 1) * BLOCK] = (
                        acc_ref[lc, pl.ds(c, PROJ_TM // r, stride=r), :].astype(BF16))

    @pl.when((j >= SB_BLOCK0) & (j < GATE_BLOCK0))
    def _():
        sb_ref[...] = project(ww_ref).astype(BF16)

    @pl.when(j >= GATE_BLOCK0)
    def _():
        gate_ref[...] = project(ww_ref).astype(BF16)


def _held(j, first, count):
    return jnp.clip(j - first, 0, count - 1)


def _norm_proj(x, g, w_dil, w_wide):
    t = x.shape[0]

    def dil_spec(group):
        r = DIL_GROUPS[group][1]
        return pl.BlockSpec((None, PROJ_TM // r, r * PROJ_TN),
                            lambda i, j: (_held(j, group * N_ROLES, N_ROLES), i, 0))

    def dil_shape(group):
        r = DIL_GROUPS[group][1]
        return jax.ShapeDtypeStruct((N_ROLES, t // r, r * GROUP_WIDTH), BF16)

    return pl.pallas_call(
        _norm_proj_kernel,
        grid=(t // PROJ_TM, N_PROJ_BLOCKS),
        in_specs=[
            pl.BlockSpec((PROJ_TM, D_MODEL), lambda i, j: (i, 0)),
            pl.BlockSpec((1, D_MODEL), lambda i, j: (0, 0)),
            pl.BlockSpec((D_MODEL, PROJ_TN), lambda i, j: (0, _held(j, 0, SB_BLOCK0))),
            pl.BlockSpec((D_MODEL, PROJ_TN_WIDE), lambda i, j: (0, _held(j, SB_BLOCK0, N_PROJ_BLOCKS - SB_BLOCK0))),
        ],
        out_specs=[
            dil_spec(0), dil_spec(1), dil_spec(2),
            pl.BlockSpec((PROJ_TM, PROJ_TN_WIDE),
                         lambda i, j: (i, _held(j, SB_BLOCK0, GATE_BLOCK0 - SB_BLOCK0))),
            pl.BlockSpec((PROJ_TM, PROJ_TN_WIDE),
                         lambda i, j: (i, _held(j, GATE_BLOCK0, N_PROJ_BLOCKS - GATE_BLOCK0))),
        ],
        out_shape=[
            dil_shape(0), dil_shape(1), dil_shape(2),
            jax.ShapeDtypeStruct((t, 3 * SB_WIDTH), BF16),
            jax.ShapeDtypeStruct((t, GATE_WIDTH), BF16),
        ],
        scratch_shapes=[pltpu.VMEM((PROJ_TM, D_MODEL), BF16),
                        pltpu.VMEM((PROJ_TN // BLOCK, PROJ_TM, BLOCK), F32)],
        compiler_params=_params(("parallel", "arbitrary")),
        name="norm_proj",
    )(x, g, w_dil, w_wide)


def _reorder_w_in(w):
    cols = [w[:, role * DIL_WIDTH + group * GROUP_WIDTH: role * DIL_WIDTH + (group + 1) * GROUP_WIDTH]
            for group in range(N_GROUPS) for role in range(N_ROLES)]
    return jnp.concatenate(cols, axis=1).astype(BF16), w[:, N_ROLES * DIL_WIDTH:].astype(BF16)


DIL_SHAPE = ((4, 1), (4, 1), (2, 2))


def _norm_rope(x, gain, cos, sin):
    y = _rms_norm(x, gain)
    return y * cos + pltpu.roll(y, HEAD_DIM // 2, 1) * sin


def _dil_kernel(q_ref, kc_ref, kp_ref, vc_ref, vp_ref, csc_ref, csp_ref, qg_ref, kg_ref,
                o_ref, ld_ref, *, r, n_sub, n_res):
    n = pl.program_id(1)
    cc = pl.program_id(2)
    qg = qg_ref[...]
    kg = kg_ref[...]

    row = lax.broadcasted_iota(jnp.int32, (BLOCK, 2 * BLOCK), 0)
    col = lax.broadcasted_iota(jnp.int32, (BLOCK, 2 * BLOCK), 1)
    off = BLOCK + row - col
    band = (off >= 0) & (off <= BLOCK)
    band_first = band & ((col >= BLOCK) | (n > 0))

    for cl in range(n_res):
        cos_sl = slice(cl * 2 * HEAD_DIM, cl * 2 * HEAD_DIM + HEAD_DIM)
        sin_sl = slice(cl * 2 * HEAD_DIM + HEAD_DIM, (cl + 1) * 2 * HEAD_DIM)
        cos_c = csc_ref[:, cos_sl]
        sin_c = csc_ref[:, sin_sl]
        cos_k = jnp.concatenate([csp_ref[:, cos_sl], cos_c], axis=0)
        sin_k = jnp.concatenate([csp_ref[:, sin_sl], sin_c], axis=0)
        for hh in range(HEADS_PER_GROUP):
            sl = slice(cl * GROUP_WIDTH + hh * HEAD_DIM, cl * GROUP_WIDTH + (hh + 1) * HEAD_DIM)
            q = _norm_rope(q_ref[:, sl].astype(F32), qg, cos_c, sin_c).astype(BF16)
            k_raw = jnp.concatenate([kp_ref[:, sl], kc_ref[:, sl]], axis=0).astype(F32)
            k = _norm_rope(k_raw, kg, cos_k, sin_k).astype(BF16)
            v = jnp.concatenate([vp_ref[:, sl], vc_ref[:, sl]], axis=0)
            for m in range(n_sub):
                rows = slice(m * BLOCK, (m + 1) * BLOCK)
                keys = slice(m * BLOCK, (m + 2) * BLOCK)
                s = lax.dot_general(q[rows], k[keys], (((1,), (1,)), ((), ())), preferred_element_type=F32)
                s = jnp.where(band_first if m == 0 else band, s * INV_SQRT_D, -jnp.inf)
                mx = jnp.max(s, axis=-1, keepdims=True)
                p = jnp.exp(s - mx)
                l = jnp.sum(p, axis=-1, keepdims=True)
                o = jnp.dot((p / l).astype(BF16), v[keys], preferred_element_type=F32)
                ld = jnp.broadcast_to(mx + jnp.log(l), (BLOCK, HEAD_DIM))
                if r == 1:
                    tok = rows
                else:
                    tok = pl.ds(m * BLOCK * r + cc * n_res + cl, BLOCK, stride=r)
                o_ref[hh, tok, :] = o
                ld_ref[hh, tok, :] = ld


def _dilated_attention(d, cs, q_gain, k_gain, group, batch, seq):
    _, r = DIL_GROUPS[group]
    n_sub, n_res = DIL_SHAPE[group]
    rows = n_sub * BLOCK
    nb = seq // r // rows
    cs_v = cs.reshape(seq // r, r * 2 * HEAD_DIM)
    width = n_res * GROUP_WIDTH

    def cur(role):
        return pl.BlockSpec((None, rows, width), lambda b, n, c: (role, b * nb + n, c))

    def prev(role):
        return pl.BlockSpec((None, BLOCK, width),
                            lambda b, n, c: (role, jnp.maximum((b * nb + n) * n_sub - 1, 0), c))

    out_spec = pl.BlockSpec((HEADS_PER_GROUP, rows * r, HEAD_DIM), lambda b, n, c: (0, b * nb + n, 0))
    out_sds = jax.ShapeDtypeStruct((HEADS_PER_GROUP, batch * seq, HEAD_DIM), F32)
    return pl.pallas_call(
        functools.partial(_dil_kernel, r=r, n_sub=n_sub, n_res=n_res),
        grid=(batch, nb, r // n_res),
        in_specs=[
            cur(0), cur(1), prev(1), cur(2), prev(2),
            pl.BlockSpec((rows, n_res * 2 * HEAD_DIM), lambda b, n, c: (n, c)),
            pl.BlockSpec((BLOCK, n_res * 2 * HEAD_DIM), lambda b, n, c: (jnp.maximum(n * n_sub - 1, 0), c)),
            pl.BlockSpec((1, HEAD_DIM), lambda b, n, c: (0, 0)),
            pl.BlockSpec((1, HEAD_DIM), lambda b, n, c: (0, 0)),
        ],
        out_specs=[out_spec, out_spec],
        out_shape=[out_sds, out_sds],
        compiler_params=_params(("parallel", "parallel", "arbitrary")),
        name=f"dilated_r{r}",
    )(d, d, d, d, d, cs_v, cs_v, q_gain, k_gain)


SB_T = 128
SB_CHAINS = 4
SB_TQ = SB_CHAINS * SB_T
SB_STATIC_TILES = 2
SB_DONE = -104.0
SB_SKIP = -1e30


def _sb_kernel(q_ref, k_ref, v_ref, o_ref):
    i = pl.program_id(2)
    r0 = i * SB_TQ
    tj = lax.broadcasted_iota(jnp.int32, (2 * SB_T, 2 * SB_T), 0) % SB_T
    ts = lax.broadcasted_iota(jnp.int32, (2 * SB_T, 2 * SB_T), 1)
    sum_rhs = jnp.where((tj >= ts) | (ts >= SB_T), 1.0, 0.0).astype(BF16)
    qi = lax.broadcasted_iota(jnp.int32, (SB_T, SB_T), 0)
    ki = lax.broadcasted_iota(jnp.int32, (SB_T, SB_T), 1)
    diag = ki < qi

    def tile(s, start, carry, acc, mask=None):
        q = q_ref[s * SB_T:(s + 1) * SB_T, :]
        k = k_ref[pl.ds(start, SB_T), :]
        v = v_ref[pl.ds(start, SB_T), :]
        z = lax.dot_general(q, k, (((1,), (1,)), ((), ())), preferred_element_type=F32) * INV_SQRT_D
        lk = jnp.minimum(-z, 0.0) - jnp.log(1.0 + jnp.exp(-jnp.abs(z)))
        if mask is not None:
            lk = jnp.where(mask, lk, 0.0)
        hi = lk.astype(BF16)
        lo = (lk - hi.astype(F32)).astype(BF16)
        sums = jnp.dot(jnp.concatenate([hi, lo], axis=1), sum_rhs, preferred_element_type=F32)
        a = jnp.exp(z + sums[:, :SB_T] + carry)
        if mask is not None:
            a = jnp.where(mask, a, 0.0)
        acc = acc + jnp.dot(a.astype(BF16), v, preferred_element_type=F32)
        return carry + sums[:, SB_T:], acc

    zeros = jnp.zeros((SB_T, SB_T), F32)
    state = [tile(s, pl.multiple_of(r0 + s * SB_T, SB_T), zeros, zeros, diag) for s in range(SB_CHAINS)]

    def step(p, state):
        out = []
        for s in range(SB_CHAINS):
            carry, acc = state[s]
            start = r0 + (s - p) * SB_T
            carry = jnp.where(start >= 0, carry, SB_SKIP)
            start = pl.multiple_of(jnp.maximum(start, 0), SB_T)
            out.append(tile(s, start, carry, acc))
        return out

    for p in range(1, SB_STATIC_TILES + 1):
        state = step(p, state)

    def unfinished(p, state):
        live = state[0][0]
        for s in range(1, SB_CHAINS):
            live = jnp.maximum(live, state[s][0])
        has_keys = r0 + (SB_CHAINS - 1 - p) * SB_T >= 0
        return has_keys & (jnp.max(live) > SB_DONE)

    def cond(loop):
        return unfinished(*loop)

    def body(loop):
        p, state = loop
        return p + 1, step(p, state)

    _, state = lax.while_loop(cond, body, (jnp.int32(SB_STATIC_TILES + 1), state))
    for s in range(SB_CHAINS):
        o_ref[s * SB_T:(s + 1) * SB_T, :] = state[s][1].astype(o_ref.dtype)


def _stick_breaking_attention(sb, batch, seq):
    qkv_v = sb.reshape(batch, seq, 3 * SB_WIDTH)
    return pl.pallas_call(
        _sb_kernel,
        grid=(batch, SB_HEADS, seq // SB_TQ),
        in_specs=[
            pl.BlockSpec((None, SB_TQ, HEAD_DIM), lambda b, h, i: (b, i, h)),
            pl.BlockSpec((None, seq, HEAD_DIM), lambda b, h, i: (b, 0, SB_HEADS + h)),
            pl.BlockSpec((None, seq, HEAD_DIM), lambda b, h, i: (b, 0, 2 * SB_HEADS + h)),
        ],
        out_specs=pl.BlockSpec((None, SB_TQ, HEAD_DIM), lambda b, h, i: (b, i, h)),
        out_shape=jax.ShapeDtypeStruct((batch, seq, SB_WIDTH), BF16),
        compiler_params=_params(("parallel", "parallel", "arbitrary")),
        name="stick_breaking",
    )(qkv_v, qkv_v, qkv_v).reshape(batch * seq, SB_WIDTH)


MIX_TM = 256


def _mix_kernel(x_ref, o0_ref, o1_ref, o2_ref, l0_ref, l1_ref, l2_ref, ysb_ref, gate_ref, gb_ref,
                wud_ref, wus_ref, wo_ref, out_ref):
    heads = []
    for hh in range(HEADS_PER_GROUP):
        l0, l1, l2 = l0_ref[hh], l1_ref[hh], l2_ref[hh]
        m = jnp.maximum(jnp.maximum(l0, l1), l2)
        e0, e1, e2 = jnp.exp(l0 - m), jnp.exp(l1 - m), jnp.exp(l2 - m)
        y = (e0 * o0_ref[hh] + e1 * o1_ref[hh] + e2 * o2_ref[hh]) / (e0 + e1 + e2)
        heads.append(y.astype(BF16))
    y_dil = jnp.concatenate(heads, axis=1)
    up_dil = jnp.dot(y_dil, wud_ref[...], preferred_element_type=F32)
    up_sb = jnp.dot(ysb_ref[...], wus_ref[...], preferred_element_type=F32)
    gates = jax.nn.sigmoid(gate_ref[...].astype(F32) + gb_ref[...])
    mixed = gates[:, :D_MODEL] * up_dil + gates[:, D_MODEL:] * up_sb
    out_ref[...] = x_ref[...] + jnp.dot(mixed.astype(BF16), wo_ref[...], preferred_element_type=F32)


def _mix(x, outs, lds, y_sb, gates, gate_b, w_up_dil, w_up_sb, w_out):
    t = x.shape[0]
    rows = lambda w: pl.BlockSpec((MIX_TM, w), lambda i: (i, 0))
    full = lambda a: pl.BlockSpec(a.shape, lambda i: (0, 0))
    per_head = pl.BlockSpec((HEADS_PER_GROUP, MIX_TM, HEAD_DIM), lambda i: (0, i, 0))
    return pl.pallas_call(
        _mix_kernel,
        grid=(t // MIX_TM,),
        in_specs=[rows(D_MODEL)] + [per_head] * 6 + [rows(SB_WIDTH), rows(GATE_WIDTH),
                                                     full(gate_b), full(w_up_dil),
                                                     full(w_up_sb), full(w_out)],
        out_specs=rows(D_MODEL),
        out_shape=jax.ShapeDtypeStruct((t, D_MODEL), F32),
        compiler_params=_params(("parallel",)),
        name="mix",
    )(x, *outs, *lds, y_sb, gates, gate_b, w_up_dil, w_up_sb, w_out)


MLP_TM = 1024
MLP_TF = 512


def _mlp_kernel(x_ref, g_ref, w1_ref, w2_ref, o_ref, h_ref):
    j = pl.program_id(1)

    @pl.when(j == 0)
    def _():
        x = x_ref[...]
        h_ref[...] = _rms_norm(x, g_ref[...]).astype(BF16)
        o_ref[...] = x

    u = jnp.maximum(jnp.dot(h_ref[...], w1_ref[...], preferred_element_type=F32), 0.0)
    o_ref[...] += jnp.dot((u * u).astype(BF16), w2_ref[...], preferred_element_type=F32)


def _mlp(x, g, w1, w2):
    t = x.shape[0]
    return pl.pallas_call(
        _mlp_kernel,
        grid=(t // MLP_TM, D_FF // MLP_TF),
        in_specs=[
            pl.BlockSpec((MLP_TM, D_MODEL), lambda i, j: (i, 0)),
            pl.BlockSpec((1, D_MODEL), lambda i, j: (0, 0)),
            pl.BlockSpec((D_MODEL, MLP_TF), lambda i, j: (0, j)),
            pl.BlockSpec((MLP_TF, D_MODEL), lambda i, j: (j, 0)),
        ],
        out_specs=pl.BlockSpec((MLP_TM, D_MODEL), lambda i, j: (i, 0)),
        out_shape=jax.ShapeDtypeStruct((t, D_MODEL), F32),
        scratch_shapes=[pltpu.VMEM((MLP_TM, D_MODEL), BF16)],
        compiler_params=_params(("parallel", "arbitrary")),
        name="mlp",
    )(x, g, w1, w2)


def _rope_table(seq):
    half = HEAD_DIM // 2
    inv_freq = ROPE_THETA ** (-jnp.arange(half, dtype=F32) / half)
    ang = jnp.arange(seq, dtype=F32)[:, None] * inv_freq[None, :]
    cos, sin = jnp.cos(ang), jnp.sin(ang)
    return jnp.concatenate([cos, cos, -sin, sin], axis=-1)


def kernel(x, norm1_g, w_in, q_norm_g, k_norm_g, w_up_dil, w_up_sb, gate_b, w_out, norm2_g, w_ff1, w_ff2):
    batch, seq, d = x.shape
    xf = x.reshape(batch * seq, d)
    cs = _rope_table(seq)
    for layer in range(DEPTH):
        *dil, sb, gates = _norm_proj(xf, norm1_g[layer][None], *_reorder_w_in(w_in[layer]))
        outs, lds = [], []
        for group in range(N_GROUPS):
            o, ld = _dilated_attention(dil[group], cs, q_norm_g[layer, group][None],
                                       k_norm_g[layer, group][None], group, batch, seq)
            outs.append(o)
            lds.append(ld)
        y_sb = _stick_breaking_attention(sb, batch, seq)
        xf = _mix(xf, outs, lds, y_sb, gates, gate_b[layer].reshape(1, GATE_WIDTH),
                  w_up_dil[layer].astype(BF16), w_up_sb[layer].astype(BF16), w_out[layer].astype(BF16))
        xf = _mlp(xf, norm2_g[layer][None], w_ff1[layer].astype(BF16), w_ff2[layer].astype(BF16))
    return xf.reshape(batch, seq, d)
```

```python
import functools
import math

import jax
import jax.numpy as jnp
from jax import lax
from jax.experimental import pallas as pl
from jax.experimental.pallas import tpu as pltpu

D_MODEL = 2048
DEPTH = 2
HEAD_DIM = 128
DIL_GROUPS = ((128, 1), (512, 4), (2048, 16))
N_GROUPS = 3
HEADS_PER_GROUP = 4
GROUP_WIDTH = HEADS_PER_GROUP * HEAD_DIM
DIL_WIDTH = N_GROUPS * GROUP_WIDTH
SB_HEADS = 8
SB_WIDTH = SB_HEADS * HEAD_DIM
GATE_WIDTH = 2 * D_MODEL
D_FF = 4 * D_MODEL
ROPE_THETA = 10000.0
BLOCK = 128
EPS = 1e-6
INV_SQRT_D = 1.0 / math.sqrt(HEAD_DIM)

F32 = jnp.float32
BF16 = jnp.bfloat16

VMEM_LIMIT = 56 * 1024 * 1024


def _params(semantics):
    return pltpu.CompilerParams(dimension_semantics=semantics, vmem_limit_bytes=VMEM_LIMIT)


def _rms_norm(x, g):
    ms = jnp.mean(x * x, axis=-1, keepdims=True)
    return x * lax.rsqrt(ms + EPS) * g


PROJ_TM = 1024
PROJ_TN = 512
N_ROLES = 3
PROJ_TN_WIDE = 1024
SB_BLOCK0 = N_GROUPS * N_ROLES
GATE_BLOCK0 = SB_BLOCK0 + 3 * SB_WIDTH // PROJ_TN_WIDE
N_PROJ_BLOCKS = GATE_BLOCK0 + GATE_WIDTH // PROJ_TN_WIDE


def _norm_proj_kernel(x_ref, g_ref, wd_ref, ww_ref, d0_ref, d1_ref, d2_ref, sb_ref, gate_ref,
                      h_ref, acc_ref):
    j = pl.program_id(1)

    @pl.when(j == 0)
    def _():
        h_ref[...] = _rms_norm(x_ref[...], g_ref[...]).astype(BF16)

    def project(w_ref=wd_ref):
        return jnp.dot(h_ref[...], w_ref[...], preferred_element_type=F32)

    @pl.when(j < N_ROLES)
    def _():
        d0_ref[...] = project().astype(BF16)

    for group, d_ref in ((1, d1_ref), (2, d2_ref)):
        r = DIL_GROUPS[group][1]

        @pl.when((j >= group * N_ROLES) & (j < (group + 1) * N_ROLES))
        def _(r=r, d_ref=d_ref):
            acc = project()
            for lc in range(PROJ_TN // BLOCK):
                acc_ref[lc] = acc[:, lc * BLOCK:(lc + 1) * BLOCK]
            for c in range(r):
                for lc in range(PROJ_TN // BLOCK):
                    d_ref[:, c * PROJ_TN + lc * BLOCK: c * PROJ_TN + (lc + 1) * BLOCK] = (
                        acc_ref[lc, pl.ds(c, PROJ_TM // r, stride=r), :].astype(BF16))

    @pl.when((j >= SB_BLOCK0) & (j < GATE_BLOCK0))
    def _():
        sb_ref[...] = project(ww_ref).astype(BF16)

    @pl.when(j >= GATE_BLOCK0)
    def _():
        gate_ref[...] = project(ww_ref).astype(BF16)


def _held(j, first, count):
    return jnp.clip(j - first, 0, count - 1)


def _norm_proj(x, g, w_dil, w_wide):
    t = x.shape[0]

    def dil_spec(group):
        r = DIL_GROUPS[group][1]
        return pl.BlockSpec((None, PROJ_TM // r, r * PROJ_TN),
                            lambda i, j: (_held(j, group * N_ROLES, N_ROLES), i, 0))

    def dil_shape(group):
        r = DIL_GROUPS[group][1]
        return jax.ShapeDtypeStruct((N_ROLES, t // r, r * GROUP_WIDTH), BF16)

    return pl.pallas_call(
        _norm_proj_kernel,
        grid=(t // PROJ_TM, N_PROJ_BLOCKS),
        in_specs=[
            pl.BlockSpec((PROJ_TM, D_MODEL), lambda i, j: (i, 0)),
            pl.BlockSpec((1, D_MODEL), lambda i, j: (0, 0)),
            pl.BlockSpec((D_MODEL, PROJ_TN), lambda i, j: (0, _held(j, 0, SB_BLOCK0))),
            pl.BlockSpec((D_MODEL, PROJ_TN_WIDE), lambda i, j: (0, _held(j, SB_BLOCK0, N_PROJ_BLOCKS - SB_BLOCK0))),
        ],
        out_specs=[
            dil_spec(0), dil_spec(1), dil_spec(2),
            pl.BlockSpec((PROJ_TM, PROJ_TN_WIDE),
                         lambda i, j: (i, _held(j, SB_BLOCK0, GATE_BLOCK0 - SB_BLOCK0))),
            pl.BlockSpec((PROJ_TM, PROJ_TN_WIDE),
                         lambda i, j: (i, _held(j, GATE_BLOCK0, N_PROJ_BLOCKS - GATE_BLOCK0))),
        ],
        out_shape=[
            dil_shape(0), dil_shape(1), dil_shape(2),
            jax.ShapeDtypeStruct((t, 3 * SB_WIDTH), BF16),
            jax.ShapeDtypeStruct((t, GATE_WIDTH), BF16),
        ],
        scratch_shapes=[pltpu.VMEM((PROJ_TM, D_MODEL), BF16),
                        pltpu.VMEM((PROJ_TN // BLOCK, PROJ_TM, BLOCK), F32)],
        compiler_params=_params(("parallel", "arbitrary")),
        name="norm_proj",
    )(x, g, w_dil, w_wide)


def _reorder_w_in(w):
    cols = [w[:, role * DIL_WIDTH + group * GROUP_WIDTH: role * DIL_WIDTH + (group + 1) * GROUP_WIDTH]
            for group in range(N_GROUPS) for role in range(N_ROLES)]
    return jnp.concatenate(cols, axis=1).astype(BF16), w[:, N_ROLES * DIL_WIDTH:].astype(BF16)


DIL_SHAPE = ((4, 1), (4, 1), (2, 2))


def _norm_rope(x, gain, cos, sin):
    y = _rms_norm(x, gain)
    return y * cos + pltpu.roll(y, HEAD_DIM // 2, 1) * sin


def _dil_kernel(q_ref, kc_ref, kp_ref, vc_ref, vp_ref, csc_ref, csp_ref, qg_ref, kg_ref,
                o_ref, ld_ref, *, r, n_sub, n_res):
    n = pl.program_id(1)
    cc = pl.program_id(2)
    qg = qg_ref[...]
    kg = kg_ref[...]

    row = lax.broadcasted_iota(jnp.int32, (BLOCK, 2 * BLOCK), 0)
    col = lax.broadcasted_iota(jnp.int32, (BLOCK, 2 * BLOCK), 1)
    off = BLOCK + row - col
    band = (off >= 0) & (off <= BLOCK)
    band_first = band & ((col >= BLOCK) | (n > 0))

    for cl in range(n_res):
        cos_sl = slice(cl * 2 * HEAD_DIM, cl * 2 * HEAD_DIM + HEAD_DIM)
        sin_sl = slice(cl * 2 * HEAD_DIM + HEAD_DIM, (cl + 1) * 2 * HEAD_DIM)
        cos_c = csc_ref[:, cos_sl]
        sin_c = csc_ref[:, sin_sl]
        cos_k = jnp.concatenate([csp_ref[:, cos_sl], cos_c], axis=0)
        sin_k = jnp.concatenate([csp_ref[:, sin_sl], sin_c], axis=0)
        for hh in range(HEADS_PER_GROUP):
            sl = slice(cl * GROUP_WIDTH + hh * HEAD_DIM, cl * GROUP_WIDTH + (hh + 1) * HEAD_DIM)
            q = _norm_rope(q_ref[:, sl].astype(F32), qg, cos_c, sin_c).astype(BF16)
            k_raw = jnp.concatenate([kp_ref[:, sl], kc_ref[:, sl]], axis=0).astype(F32)
            k = _norm_rope(k_raw, kg, cos_k, sin_k).astype(BF16)
            v = jnp.concatenate([vp_ref[:, sl], vc_ref[:, sl]], axis=0)
            for m in range(n_sub):
                rows = slice(m * BLOCK, (m + 1) * BLOCK)
                keys = slice(m * BLOCK, (m + 2) * BLOCK)
                s = lax.dot_general(q[rows], k[keys], (((1,), (1,)), ((), ())), preferred_element_type=F32)
                s = jnp.where(band_first if m == 0 else band, s * INV_SQRT_D, -jnp.inf)
                mx = jnp.max(s, axis=-1, keepdims=True)
                p = jnp.exp(s - mx)
                l = jnp.sum(p, axis=-1, keepdims=True)
                o = jnp.dot((p / l).astype(BF16), v[keys], preferred_element_type=F32)
                ld = jnp.broadcast_to(mx + jnp.log(l), (BLOCK, HEAD_DIM))
                if r == 1:
                    tok = rows
                else:
                    tok = pl.ds(m * BLOCK * r + cc * n_res + cl, BLOCK, stride=r)
                o_ref[hh, tok, :] = o
                ld_ref[hh, tok, :] = ld


def _dilated_attention(d, cs, q_gain, k_gain, group, batch, seq):
    _, r = DIL_GROUPS[group]
    n_sub, n_res = DIL_SHAPE[group]
    rows = n_sub * BLOCK
    nb = seq // r // rows
    cs_v = cs.reshape(seq // r, r * 2 * HEAD_DIM)
    width = n_res * GROUP_WIDTH

    def cur(role):
        return pl.BlockSpec((None, rows, width), lambda b, n, c: (role, b * nb + n, c))

    def prev(role):
        return pl.BlockSpec((None, BLOCK, width),
                            lambda b, n, c: (role, jnp.maximum((b * nb + n) * n_sub - 1, 0), c))

    out_spec = pl.BlockSpec((HEADS_PER_GROUP, rows * r, HEAD_DIM), lambda b, n, c: (0, b * nb + n, 0))
    out_sds = jax.ShapeDtypeStruct((HEADS_PER_GROUP, batch * seq, HEAD_DIM), F32)
    return pl.pallas_call(
        functools.partial(_dil_kernel, r=r, n_sub=n_sub, n_res=n_res),
        grid=(batch, nb, r // n_res),
        in_specs=[
            cur(0), cur(1), prev(1), cur(2), prev(2),
            pl.BlockSpec((rows, n_res * 2 * HEAD_DIM), lambda b, n, c: (n, c)),
            pl.BlockSpec((BLOCK, n_res * 2 * HEAD_DIM), lambda b, n, c: (jnp.maximum(n * n_sub - 1, 0), c)),
            pl.BlockSpec((1, HEAD_DIM), lambda b, n, c: (0, 0)),
            pl.BlockSpec((1, HEAD_DIM), lambda b, n, c: (0, 0)),
        ],
        out_specs=[out_spec, out_spec],
        out_shape=[out_sds, out_sds],
        compiler_params=_params(("parallel", "parallel", "arbitrary")),
        name=f"dilated_r{r}",
    )(d, d, d, d, d, cs_v, cs_v, q_gain, k_gain)


SB_T = 128
SB_CHAINS = 8
SB_TQ = SB_CHAINS * SB_T
SB_STATIC_TILES = 2
SB_DONE = -104.0
SB_SKIP = -1e30


def _sb_kernel(q_ref, k_ref, v_ref, o_ref):
    i = pl.program_id(2)
    r0 = i * SB_TQ
    tj = lax.broadcasted_iota(jnp.int32, (2 * SB_T, 2 * SB_T), 0) % SB_T
    ts = lax.broadcasted_iota(jnp.int32, (2 * SB_T, 2 * SB_T), 1)
    sum_rhs = jnp.where((tj >= ts) | (ts >= SB_T), 1.0, 0.0).astype(BF16)
    qi = lax.broadcasted_iota(jnp.int32, (SB_T, SB_T), 0)
    ki = lax.broadcasted_iota(jnp.int32, (SB_T, SB_T), 1)
    diag = ki < qi

    def tile(s, start, carry, acc, mask=None):
        q = q_ref[s * SB_T:(s + 1) * SB_T, :]
        k = k_ref[pl.ds(start, SB_T), :]
        v = v_ref[pl.ds(start, SB_T), :]
        z = lax.dot_general(q, k, (((1,), (1,)), ((), ())), preferred_element_type=F32) * INV_SQRT_D
        lk = jnp.minimum(-z, 0.0) - jnp.log(1.0 + jnp.exp(-jnp.abs(z)))
        if mask is not None:
            lk = jnp.where(mask, lk, 0.0)
        hi = lk.astype(BF16)
        lo = (lk - hi.astype(F32)).astype(BF16)
        sums = jnp.dot(jnp.concatenate([hi, lo], axis=1), sum_rhs, preferred_element_type=F32)
        a = jnp.exp(z + sums[:, :SB_T] + carry)
        if mask is not None:
            a = jnp.where(mask, a, 0.0)
        acc = acc + jnp.dot(a.astype(BF16), v, preferred_element_type=F32)
        return carry + sums[:, SB_T:], acc

    zeros = jnp.zeros((SB_T, SB_T), F32)
    state = [tile(s, pl.multiple_of(r0 + s * SB_T, SB_T), zeros, zeros, diag) for s in range(SB_CHAINS)]

    def step(p, state):
        out = []
        for s in range(SB_CHAINS):
            carry, acc = state[s]
            start = r0 + (s - p) * SB_T
            carry = jnp.where(start >= 0, carry, SB_SKIP)
            start = pl.multiple_of(jnp.maximum(start, 0), SB_T)
            out.append(tile(s, start, carry, acc))
        return out

    for p in range(1, SB_STATIC_TILES + 1):
        state = step(p, state)

    def unfinished(p, state):
        live = state[0][0]
        for s in range(1, SB_CHAINS):
            live = jnp.maximum(live, state[s][0])
        has_keys = r0 + (SB_CHAINS - 1 - p) * SB_T >= 0
        return has_keys & (jnp.max(live) > SB_DONE)

    def cond(loop):
        return unfinished(*loop)

    def body(loop):
        p, state = loop
        return p + 1, step(p, state)

    _, state = lax.while_loop(cond, body, (jnp.int32(SB_STATIC_TILES + 1), state))
    for s in range(SB_CHAINS):
        o_ref[s * SB_T:(s + 1) * SB_T, :] = state[s][1].astype(o_ref.dtype)


def _stick_breaking_attention(sb, batch, seq):
    qkv_v = sb.reshape(batch, seq, 3 * SB_WIDTH)
    return pl.pallas_call(
        _sb_kernel,
        grid=(batch, SB_HEADS, seq // SB_TQ),
        in_specs=[
            pl.BlockSpec((None, SB_TQ, HEAD_DIM), lambda b, h, i: (b, i, h)),
            pl.BlockSpec((None, seq, HEAD_DIM), lambda b, h, i: (b, 0, SB_HEADS + h)),
            pl.BlockSpec((None, seq, HEAD_DIM), lambda b, h, i: (b, 0, 2 * SB_HEADS + h)),
        ],
        out_specs=pl.BlockSpec((None, SB_TQ, HEAD_DIM), lambda b, h, i: (b, i, h)),
        out_shape=jax.ShapeDtypeStruct((batch, seq, SB_WIDTH), BF16),
        compiler_params=_params(("parallel", "parallel", "arbitrary")),
        name="stick_breaking",
    )(qkv_v, qkv_v, qkv_v).reshape(batch * seq, SB_WIDTH)


MIX_TM = 256


def _mix_kernel(x_ref, o0_ref, o1_ref, o2_ref, l0_ref, l1_ref, l2_ref, ysb_ref, gate_ref, gb_ref,
                wud_ref, wus_ref, wo_ref, out_ref):
    heads = []
    for hh in range(HEADS_PER_GROUP):
        l0, l1, l2 = l0_ref[hh], l1_ref[hh], l2_ref[hh]
        m = jnp.maximum(jnp.maximum(l0, l1), l2)
        e0, e1, e2 = jnp.exp(l0 - m), jnp.exp(l1 - m), jnp.exp(l2 - m)
        y = (e0 * o0_ref[hh] + e1 * o1_ref[hh] + e2 * o2_ref[hh]) / (e0 + e1 + e2)
        heads.append(y.astype(BF16))
    y_dil = jnp.concatenate(heads, axis=1)
    up_dil = jnp.dot(y_dil, wud_ref[...], preferred_element_type=F32)
    up_sb = jnp.dot(ysb_ref[...], wus_ref[...], preferred_element_type=F32)
    gates = jax.nn.sigmoid(gate_ref[...].astype(F32) + gb_ref[...])
    mixed = gates[:, :D_MODEL] * up_dil + gates[:, D_MODEL:] * up_sb
    out_ref[...] = x_ref[...] + jnp.dot(mixed.astype(BF16), wo_ref[...], preferred_element_type=F32)


def _mix(x, outs, lds, y_sb, gates, gate_b, w_up_dil, w_up_sb, w_out):
    t = x.shape[0]
    rows = lambda w: pl.BlockSpec((MIX_TM, w), lambda i: (i, 0))
    full = lambda a: pl.BlockSpec(a.shape, lambda i: (0, 0))
    per_head = pl.BlockSpec((HEADS_PER_GROUP, MIX_TM, HEAD_DIM), lambda i: (0, i, 0))
    return pl.pallas_call(
        _mix_kernel,
        grid=(t // MIX_TM,),
        in_specs=[rows(D_MODEL)] + [per_head] * 6 + [rows(SB_WIDTH), rows(GATE_WIDTH),
                                                     full(gate_b), full(w_up_dil),
                                                     full(w_up_sb), full(w_out)],
        out_specs=rows(D_MODEL),
        out_shape=jax.ShapeDtypeStruct((t, D_MODEL), F32),
        compiler_params=_params(("parallel",)),
        name="mix",
    )(x, *outs, *lds, y_sb, gates, gate_b, w_up_dil, w_up_sb, w_out)


MLP_TM = 1024
MLP_TF = 512


def _mlp_kernel(x_ref, g_ref, w1_ref, w2_ref, o_ref, h_ref):
    j = pl.program_id(1)

    @pl.when(j == 0)
    def _():
        x = x_ref[...]
        h_ref[...] = _rms_norm(x, g_ref[...]).astype(BF16)
        o_ref[...] = x

    u = jnp.maximum(jnp.dot(h_ref[...], w1_ref[...], preferred_element_type=F32), 0.0)
    o_ref[...] += jnp.dot((u * u).astype(BF16), w2_ref[...], preferred_element_type=F32)


def _mlp(x, g, w1, w2):
    t = x.shape[0]
    return pl.pallas_call(
        _mlp_kernel,
        grid=(t // MLP_TM, D_FF // MLP_TF),
        in_specs=[
            pl.BlockSpec((MLP_TM, D_MODEL), lambda i, j: (i, 0)),
            pl.BlockSpec((1, D_MODEL), lambda i, j: (0, 0)),
            pl.BlockSpec((D_MODEL, MLP_TF), lambda i, j: (0, j)),
            pl.BlockSpec((MLP_TF, D_MODEL), lambda i, j: (j, 0)),
        ],
        out_specs=pl.BlockSpec((MLP_TM, D_MODEL), lambda i, j: (i, 0)),
        out_shape=jax.ShapeDtypeStruct((t, D_MODEL), F32),
        scratch_shapes=[pltpu.VMEM((MLP_TM, D_MODEL), BF16)],
        compiler_params=_params(("parallel", "arbitrary")),
        name="mlp",
    )(x, g, w1, w2)


def _rope_table(seq):
    half = HEAD_DIM // 2
    inv_freq = ROPE_THETA ** (-jnp.arange(half, dtype=F32) / half)
    ang = jnp.arange(seq, dtype=F32)[:, None] * inv_freq[None, :]
    cos, sin = jnp.cos(ang), jnp.sin(ang)
    return jnp.concatenate([cos, cos, -sin, sin], axis=-1)


def kernel(x, norm1_g, w_in, q_norm_g, k_norm_g, w_up_dil, w_up_sb, gate_b, w_out, norm2_g, w_ff1, w_ff2):
    batch, seq, d = x.shape
    xf = x.reshape(batch * seq, d)
    cs = _rope_table(seq)
    for layer in range(DEPTH):
        *dil, sb, gates = _norm_proj(xf, norm1_g[layer][None], *_reorder_w_in(w_in[layer]))
        outs, lds = [], []
        for group in range(N_GROUPS):
            o, ld = _dilated_attention(dil[group], cs, q_norm_g[layer, group][None],
                                       k_norm_g[layer, group][None], group, batch, seq)
            outs.append(o)
            lds.append(ld)
        y_sb = _stick_breaking_attention(sb, batch, seq)
        xf = _mix(xf, outs, lds, y_sb, gates, gate_b[layer].reshape(1, GATE_WIDTH),
                  w_up_dil[layer].astype(BF16), w_up_sb[layer].astype(BF16), w_out[layer].astype(BF16))
        xf = _mlp(xf, norm2_g[layer][None], w_ff1[layer].astype(BF16), w_ff2[layer].astype(BF16))
    return xf.reshape(batch, seq, d)
```
